```python
import math
import jax, jax.numpy as jnp
from jax import lax
import numpy as np

D_MODEL = 1024
BATCH = 16
SEQ = 2048
DEPTH = 1
DEC_BATCH = 32
DEC_SEQ = 64
PAST_LEN = 1024

CHUNK = 64
QBLOCK = 128
GLA_HEADS = 4
GLA_DK = D_MODEL // 8
GLA_DV = D_MODEL // 4
GLA_LOW_RANK = 16
GLA_TAU = 16.0
SB_HEADS = 16
SB_DH = D_MODEL // SB_HEADS
N_GROUPS = 4
EXPERTS_PER_GROUP = 8
N_EXPERTS = N_GROUPS * EXPERTS_PER_GROUP
TOP_K_IN_GROUP = 2
EXPERT_FF = D_MODEL // 2
DEEPNORM_ALPHA = (2.0 * DEPTH) ** 0.25
DEEPNORM_BETA = (8.0 * DEPTH) ** -0.25
LN_EPS = 1e-5

GLA_QK = GLA_HEADS * GLA_DK
GLA_V = GLA_HEADS * GLA_DV
SB_W = SB_HEADS * SB_DH
IN_SPLITS = (GLA_QK, GLA_QK, GLA_V, GLA_V, GLA_LOW_RANK, SB_W, SB_W, SB_W, D_MODEL, D_MODEL)
IN_WIDTH = sum(IN_SPLITS)
IN_OFFSETS = tuple(int(o) for o in np.cumsum(IN_SPLITS)[:-1])

kernel_name = "streaming_gla_stickbreak_hmoe"


def layer_norm(x, g=None, b=None):
    xf = x.astype(jnp.float32)
    mu = xf.mean(-1, keepdims=True)
    var = jnp.square(xf - mu).mean(-1, keepdims=True)
    y = (xf - mu) * lax.rsqrt(var + LN_EPS)
    if g is not None:
        y = y * g.astype(jnp.float32) + b.astype(jnp.float32)
    return y.astype(x.dtype)


def rms_norm_heads(o, g):
    of = o.astype(jnp.float32)
    y = of * lax.rsqrt(jnp.mean(jnp.square(of), -1, keepdims=True) + LN_EPS) * g.astype(jnp.float32)
    return y.astype(o.dtype)


def gla_chunked(q, k, v, log_a, s0):
    bsz, t, h, dk = q.shape
    dv = v.shape[-1]
    c = CHUNK if t % CHUNK == 0 else t
    n = t // c

    def to_chunks(a):
        return a.reshape(bsz, n, c, h, a.shape[-1]).transpose(1, 0, 2, 3, 4)

    causal = jnp.tril(jnp.ones((c, c), dtype=bool))

    def step(s, inp):
        qc, kc, vc, lc = inp
        qf, kf, vf = qc.astype(jnp.float32), kc.astype(jnp.float32), vc.astype(jnp.float32)
        b = jnp.cumsum(lc, axis=1)
        o_inter = jnp.einsum('bthk,bhkv->bthv', qf * jnp.exp(b), s)
        diff = b[:, :, None] - b[:, None, :]
        decay = jnp.exp(jnp.where(causal[None, :, :, None, None], diff, -jnp.inf))
        scores = jnp.einsum('bthk,bshk,btshk->bhts', qf, kf, decay)
        o_intra = jnp.einsum('bhts,bshv->bthv', scores, vf)
        b_last = b[:, -1]
        s_new = jnp.exp(b_last)[..., None] * s + jnp.einsum(
            'bshk,bshv->bhkv', kf * jnp.exp(b_last[:, None] - b), vf)
        return s_new, o_inter + o_intra

    s_fin, o = lax.scan(step, s0.astype(jnp.float32),
                        (to_chunks(q), to_chunks(k), to_chunks(v), to_chunks(log_a)))
    o = o.transpose(1, 0, 2, 3, 4).reshape(bsz, t, h, dv)
    return o.astype(v.dtype), s_fin.astype(s0.dtype)


def stick_breaking(q, k, v, q_off):
    bsz, tq, h, d = q.shape
    tk = k.shape[1]
    blk = QBLOCK if tq % QBLOCK == 0 else tq
    nb = tq // blk
    qb = q.reshape(bsz, nb, blk, h, d).transpose(1, 0, 2, 3, 4)
    kpos = jnp.arange(tk)
    scale = d ** -0.5

    def one_block(args):
        qi, i = args
        qpos = q_off + i * blk + jnp.arange(blk)
        z = jnp.einsum('bqhd,bkhd->bhqk', qi, k).astype(jnp.float32) * scale
        mask = (kpos[None, :] < qpos[:, None])[None, None]
        log_1mb = jnp.where(mask, jax.nn.log_sigmoid(-z), 0.0)
        rest = lax.cumsum(log_1mb, axis=3, reverse=True) - log_1mb
        weights = jnp.exp(jnp.where(mask, jax.nn.log_sigmoid(z) + rest, -jnp.inf))
        return jnp.einsum('bhqk,bkhd->bqhd', weights.astype(v.dtype), v)

    out = lax.map(one_block, (qb, jnp.arange(nb)))
    return out.transpose(1, 0, 2, 3, 4).reshape(bsz, tq, h, d)


def token_mixer(h, sb_k_past, sb_v_past, gla_s0, p):
    bsz, t, _ = h.shape
    proj = h @ p['w_in']
    (qa, ka, va, ga, lra, qb, kb, vb, gate_a, gate_b) = jnp.split(proj, IN_OFFSETS, axis=-1)
    qa = qa.reshape(bsz, t, GLA_HEADS, GLA_DK) * (GLA_DK ** -0.5)
    ka = ka.reshape(bsz, t, GLA_HEADS, GLA_DK)
    va = va.reshape(bsz, t, GLA_HEADS, GLA_DV)
    log_a = jax.nn.log_sigmoid((lra @ p['w_gla_a2'] + p['b_gla_a']).astype(jnp.float32)) / GLA_TAU
    log_a = log_a.reshape(bsz, t, GLA_HEADS, GLA_DK)
    oa, s_new = gla_chunked(qa, ka, va, log_a, gla_s0)
    oa = rms_norm_heads(oa, p['gla_norm_g']).reshape(bsz, t, GLA_V) * jax.nn.silu(ga)
    qb = qb.reshape(bsz, t, SB_HEADS, SB_DH)
    kb = kb.reshape(bsz, t, SB_HEADS, SB_DH)
    vb = vb.reshape(bsz, t, SB_HEADS, SB_DH)
    k_all = jnp.concatenate([sb_k_past.astype(kb.dtype), kb], axis=1)
    v_all = jnp.concatenate([sb_v_past.astype(vb.dtype), vb], axis=1)
    ob = stick_breaking(qb, k_all, v_all, sb_k_past.shape[1]).reshape(bsz, t, SB_W)
    merged = (jax.nn.sigmoid(gate_a) * (oa @ p['w_branch_a'])
              + jax.nn.sigmoid(gate_b) * (ob @ p['w_branch_b']))
    return merged @ p['w_out'], kb, vb, s_new


def hier_moe(h, p):
    bsz, t, d = h.shape
    tok = h.reshape(-1, d)
    gl = (tok @ p['w_group'] + p['b_group']).astype(jnp.float32)
    pg = jax.nn.softmax(gl, axis=-1)
    gi = jnp.argmax(gl, axis=-1)
    p_top = jnp.take_along_axis(pg, gi[:, None], axis=1)[:, 0]
    el = (tok @ p['w_router'] + p['b_router']).astype(jnp.float32).reshape(-1, N_GROUPS, EXPERTS_PER_GROUP)
    sel = jnp.take_along_axis(el, gi[:, None, None], axis=1)[:, 0]
    vals, idx = lax.top_k(sel, TOP_K_IN_GROUP)
    w2 = jax.nn.softmax(vals, axis=-1) * p_top[:, None]
    eid = gi[:, None] * EXPERTS_PER_GROUP + idx
    combine = jnp.einsum('nk,nke->ne', w2, jax.nn.one_hot(eid, N_EXPERTS, dtype=jnp.float32))
    y = jnp.zeros(tok.shape, jnp.float32)
    for e in range(N_EXPERTS):
        he = jax.nn.silu(tok @ p['w_exp_gate'][e]) * (tok @ p['w_exp_up'][e])
        y = y + combine[:, e:e + 1] * (he @ p['w_exp_down'][e]).astype(jnp.float32)
    return y.astype(h.dtype).reshape(bsz, t, d)


def trunk_layer(x, c, sb_k_past, sb_v_past, gla_s0, p):
    mod = jax.nn.silu(c) @ p['w_cond'] + p['b_cond']
    shift_m, scale_m, gate_m, shift_f, scale_f, gate_f = jnp.split(mod, 6, axis=-1)
    h = layer_norm(x) * (1 + scale_m[:, None]) + shift_m[:, None]
    mix, k_new, v_new, s_new = token_mixer(h, sb_k_past, sb_v_past, gla_s0, p)
    x = layer_norm(DEEPNORM_ALPHA * x + gate_m[:, None] * mix, p['ln1_g'], p['ln1_b'])
    h = layer_norm(x) * (1 + scale_f[:, None]) + shift_f[:, None]
    x = layer_norm(DEEPNORM_ALPHA * x + gate_f[:, None] * hier_moe(h, p), p['ln2_g'], p['ln2_b'])
    return x, k_new, v_new, s_new


def setup_inputs(seed: int = 0) -> dict:
    key = jax.random.key(seed)
    ks = jax.random.split(key, 32)
    f32 = jnp.float32
    nrm = lambda k, shape, s: jax.random.normal(k, shape, f32) * s
    D = D_MODEL
    return {
        "x_prompt": nrm(ks[0], (BATCH, SEQ, D), 1.0),
        "x_sample": nrm(ks[1], (DEC_BATCH, DEC_SEQ, D), 1.0),
        "cache_sb_k": nrm(ks[2], (DEPTH, DEC_BATCH, PAST_LEN, SB_HEADS, SB_DH), 1.0),
        "cache_sb_v": nrm(ks[3], (DEPTH, DEC_BATCH, PAST_LEN, SB_HEADS, SB_DH), 1.0),
        "state_gla": nrm(ks[4], (DEPTH, DEC_BATCH, GLA_HEADS, GLA_DK, GLA_DV), 1.0),
        "c_prompt": nrm(ks[5], (BATCH, D), 1.0),
        "c_sample": nrm(ks[6], (DEC_BATCH, D), 1.0),
        "w_cond": nrm(ks[7], (DEPTH, D, 6 * D), 0.5 * D ** -0.5),
        "b_cond": nrm(ks[8], (DEPTH, 6 * D), 0.02),
        "w_in": nrm(ks[9], (DEPTH, D, IN_WIDTH), D ** -0.5),
        "w_gla_a2": nrm(ks[10], (DEPTH, GLA_LOW_RANK, GLA_QK), GLA_LOW_RANK ** -0.5),
        "b_gla_a": nrm(ks[11], (DEPTH, GLA_QK), 0.1),
        "gla_norm_g": 1.0 + nrm(ks[12], (DEPTH, GLA_HEADS, GLA_DV), 0.02),
        "w_branch_a": nrm(ks[13], (DEPTH, GLA_V, D), DEEPNORM_BETA * GLA_V ** -0.5),
        "w_branch_b": nrm(ks[14], (DEPTH, SB_W, D), DEEPNORM_BETA * SB_W ** -0.5),
        "w_out": nrm(ks[15], (DEPTH, D, D), DEEPNORM_BETA * D ** -0.5),
        "ln1_g": 1.0 + nrm(ks[16], (DEPTH, D), 0.02),
        "ln1_b": nrm(ks[17], (DEPTH, D), 0.02),
        "w_group": nrm(ks[18], (DEPTH, D, N_GROUPS), D ** -0.5),
        "b_group": nrm(ks[19], (DEPTH, N_GROUPS), 0.01),
        "w_router": nrm(ks[20], (DEPTH, D, N_EXPERTS), D ** -0.5),
        "b_router": nrm(ks[21], (DEPTH, N_EXPERTS), 0.01),
        "w_exp_gate": nrm(ks[22], (DEPTH, N_EXPERTS, D, EXPERT_FF), D ** -0.5),
        "w_exp_up": nrm(ks[23], (DEPTH, N_EXPERTS, D, EXPERT_FF), D ** -0.5),
        "w_exp_down": nrm(ks[24], (DEPTH, N_EXPERTS, EXPERT_FF, D), DEEPNORM_BETA * EXPERT_FF ** -0.5),
        "ln2_g": 1.0 + nrm(ks[25], (DEPTH, D), 0.02),
        "ln2_b": nrm(ks[26], (DEPTH, D), 0.02),
    }


def reference(x_prompt, x_sample, cache_sb_k, cache_sb_v, state_gla, c_prompt, c_sample,
              w_cond, b_cond, w_in, w_gla_a2, b_gla_a, gla_norm_g, w_branch_a, w_branch_b,
              w_out, ln1_g, ln1_b, w_group, b_group, w_router, b_router,
              w_exp_gate, w_exp_up, w_exp_down, ln2_g, ln2_b):
    yp, ys = x_prompt, x_sample
    kp_l, vp_l, sp_l, ks_l, vs_l, ss_l = [], [], [], [], [], []
    bp = x_prompt.shape[0]
    for l in range(DEPTH):
        p = dict(w_cond=w_cond[l], b_cond=b_cond[l], w_in=w_in[l], w_gla_a2=w_gla_a2[l],
                 b_gla_a=b_gla_a[l], gla_norm_g=gla_norm_g[l], w_branch_a=w_branch_a[l],
                 w_branch_b=w_branch_b[l], w_out=w_out[l], ln1_g=ln1_g[l], ln1_b=ln1_b[l],
                 w_group=w_group[l], b_group=b_group[l], w_router=w_router[l],
                 b_router=b_router[l], w_exp_gate=w_exp_gate[l], w_exp_up=w_exp_up[l],
                 w_exp_down=w_exp_down[l], ln2_g=ln2_g[l], ln2_b=ln2_b[l])
        empty = jnp.zeros((bp, 0, SB_HEADS, SB_DH), yp.dtype)
        s0 = jnp.zeros((bp, GLA_HEADS, GLA_DK, GLA_DV), yp.dtype)
        yp, kp, vp, sp = trunk_layer(yp, c_prompt, empty, empty, s0, p)
        ys, kss, vss, sss = trunk_layer(ys, c_sample, cache_sb_k[l], cache_sb_v[l], state_gla[l], p)
        kp_l.append(kp); vp_l.append(vp); sp_l.append(sp)
        ks_l.append(kss); vs_l.append(vss); ss_l.append(sss)
    return (yp, ys, jnp.stack(kp_l), jnp.stack(vp_l), jnp.stack(sp_l),
            jnp.stack(ks_l), jnp.stack(vs_l), jnp.stack(ss_l))
```

```python
import functools
import math

import jax
import jax.numpy as jnp
from jax import lax
from jax.experimental import pallas as pl
from jax.experimental.pallas import tpu as pltpu

F32 = jnp.float32
BF16 = jnp.bfloat16
I32 = jnp.int32

LN_EPS = 1e-5
GLA_HEADS = 4
GLA_CHUNK = 64
GLA_TAU = 16.0
GLA_LOW_RANK = 16
SB_HEADS = 16
N_GROUPS = 4
EXPERTS_PER_GROUP = 8
N_EXPERTS = N_GROUPS * EXPERTS_PER_GROUP
LANES = 128
VMEM_LIMIT = 56 * 1024 * 1024
FFN_TILE = 256
NEG_BIG = -1e30


def _cparams(sem, vmem=VMEM_LIMIT):
    return pltpu.CompilerParams(dimension_semantics=sem, vmem_limit_bytes=vmem)


def _iota(shape, dim):
    return lax.broadcasted_iota(I32, shape, dim)


def _sigmoid(x):
    return 1.0 / (1.0 + jnp.exp(-x))


def _silu(x):
    return x * _sigmoid(x)


def _softplus(x):
    return jnp.maximum(x, 0.0) + jnp.log1p(jnp.exp(-jnp.abs(x)))


def _ln(x):
    mu = jnp.mean(x, axis=-1, keepdims=True)
    xc = x - mu
    var = jnp.mean(xc * xc, axis=-1, keepdims=True)
    return xc * lax.rsqrt(var + LN_EPS)


def _split_bf16(x):
    hi = x.astype(BF16)
    lo = (x - hi.astype(F32)).astype(BF16)
    return hi, lo


def _dot(a, b):
    return jnp.dot(a, b, preferred_element_type=F32)


def _dot_nt(a, b):
    return lax.dot_general(a, b, (((1,), (1,)), ((), ())), preferred_element_type=F32)


def _dot_tn(a, b):
    return lax.dot_general(a, b, (((0,), (0,)), ((), ())), preferred_element_type=F32)


def _cond_body(c_ref, w_ref, b_ref, o_ref):
    c = _silu(c_ref[...]).astype(BF16)
    o_ref[...] = _dot(c, w_ref[...]) + b_ref[...]


def _cond(c, w_bf, b):
    rows, d = c.shape
    n = w_bf.shape[1]
    tn = 1024
    return pl.pallas_call(
        _cond_body,
        out_shape=jax.ShapeDtypeStruct((rows, n), F32),
        grid=(n // tn,),
        in_specs=[pl.BlockSpec((rows, d), lambda j: (0, 0)),
                  pl.BlockSpec((d, tn), lambda j: (0, j)),
                  pl.BlockSpec((1, tn), lambda j: (0, j))],
        out_specs=pl.BlockSpec((rows, tn), lambda j: (0, j)),
        compiler_params=_cparams(("arbitrary",)),
        name="cond",
    )(c, w_bf, b.reshape(1, n))


def _lnmod_body(x_ref, sc_ref, sh_ref, o_ref):
    x = x_ref[...]
    h = _ln(x) * (1.0 + sc_ref[...]) + sh_ref[...]
    o_ref[...] = h.astype(o_ref.dtype)


def _row_tiles(bsz, t, target):
    if t >= target:
        return 1, target
    return min(bsz, target // t), t


def _lnmod(x, scale, shift):
    bsz, t, d = x.shape
    tb, tt = _row_tiles(bsz, t, 512)
    return pl.pallas_call(
        _lnmod_body,
        out_shape=jax.ShapeDtypeStruct((bsz, t, d), BF16),
        grid=(bsz // tb, t // tt),
        in_specs=[pl.BlockSpec((tb, tt, d), lambda i, j: (i, j, 0)),
                  pl.BlockSpec((tb, 1, d), lambda i, j: (i, 0, 0)),
                  pl.BlockSpec((tb, 1, d), lambda i, j: (i, 0, 0))],
        out_specs=pl.BlockSpec((tb, tt, d), lambda i, j: (i, j, 0)),
        compiler_params=_cparams(("arbitrary", "arbitrary")),
        name="lnmod",
    )(x, scale.reshape(bsz, 1, d), shift.reshape(bsz, 1, d))


def _mm_body(x_ref, w_ref, o_ref):
    o_ref[...] = _dot(x_ref[...], w_ref[...]).astype(o_ref.dtype)


def _matmul(x, w, out_dtype, name):
    n, k = x.shape
    m = w.shape[1]
    tm = min(n, 1024)
    tn = 2048 if m % 2048 == 0 else (1024 if m % 1024 == 0 else m)
    return pl.pallas_call(
        _mm_body,
        out_shape=jax.ShapeDtypeStruct((n, m), out_dtype),
        grid=(n // tm, m // tn),
        in_specs=[pl.BlockSpec((tm, k), lambda i, j: (i, 0)),
                  pl.BlockSpec((k, tn), lambda i, j: (0, j))],
        out_specs=pl.BlockSpec((tm, tn), lambda i, j: (i, j)),
        compiler_params=_cparams(("arbitrary", "arbitrary")),
        name=name,
    )(x, w)


def _gla_body(q_ref, k_ref, v_ref, g_ref, lr_ref, w2_ref, b2_ref, ng_ref, s0_ref,
              o_ref, sout_ref, st_ref, *, n_chunks, dk, dv):
    j = pl.program_id(1)
    nh = GLA_HEADS
    c = GLA_CHUNK

    @pl.when(j == 0)
    def _():
        st_ref[...] = s0_ref[0]

    row = _iota((c, c), 0)
    col = _iota((c, c), 1)
    tril = (row >= col)
    tril_bf = jnp.where(tril, 1.0, 0.0).astype(BF16)
    eye_k = _iota((dk, dk), 0) == _iota((dk, dk), 1)

    def chunk(ci, carry):
        r0 = pl.multiple_of(ci * c, c)
        sl = pl.ds(r0, c)
        lr = lr_ref[0, sl, :].astype(BF16)
        pre = _dot(lr, w2_ref[...]) + b2_ref[...]
        la = -_softplus(-pre) * (1.0 / GLA_TAU)
        hi, lo = _split_bf16(la)
        bc = _dot(tril_bf, hi) + _dot(tril_bf, lo)
        b_last = bc[c - 1:c, :]
        q = q_ref[0, sl, :].astype(F32) * (dk ** -0.5)
        k = k_ref[0, sl, :].astype(F32)
        qe = (q * jnp.exp(bc)).astype(BF16)
        ke = (k * jnp.exp(-bc)).astype(BF16)
        kd = (k * jnp.exp(b_last - bc)).astype(BF16)
        e_last = jnp.exp(b_last)
        for h in range(nh):
            ks = slice(h * dk, (h + 1) * dk)
            vs = slice(h * dv, (h + 1) * dv)
            v_h = v_ref[0, sl, vs]
            s_h = st_ref[h]
            o_inter = _dot(qe[:, ks], s_h.astype(BF16))
            sc = _dot_nt(qe[:, ks], ke[:, ks])
            sc = jnp.where(tril, sc, 0.0).astype(BF16)
            o = o_inter + _dot(sc, v_h)
            e_col = jnp.sum(jnp.where(eye_k, jnp.broadcast_to(e_last[:, ks], (dk, dk)), 0.0),
                            axis=1, keepdims=True)
            st_ref[h] = e_col * s_h + _dot_tn(kd[:, ks], v_h)
            ms = jnp.mean(o * o, axis=-1, keepdims=True)
            on = o * lax.rsqrt(ms + LN_EPS) * ng_ref[:, vs]
            gate = g_ref[0, sl, vs].astype(F32)
            o_ref[0, sl, vs] = (on * _silu(gate)).astype(o_ref.dtype)
        return carry

    lax.fori_loop(0, n_chunks, chunk, 0)

    @pl.when(j == pl.num_programs(1) - 1)
    def _():
        sout_ref[0] = st_ref[...]


def _gla(proj, lra, w2p, b2, ng, s0, *, q_col, k_col, v_col, g_col):
    bsz, t, _ = proj.shape
    nh, dk, dv = s0.shape[1], s0.shape[2], s0.shape[3]
    tt = min(t, 256)
    n_chunks = tt // GLA_CHUNK
    body = functools.partial(_gla_body, n_chunks=n_chunks, dk=dk, dv=dv)
    return pl.pallas_call(
        body,
        out_shape=(jax.ShapeDtypeStruct((bsz, t, nh * dv), BF16),
                   jax.ShapeDtypeStruct((bsz, nh, dk, dv), F32)),
        grid=(bsz, t // tt),
        in_specs=[pl.BlockSpec((1, tt, nh * dk), lambda b, j: (b, j, q_col)),
                  pl.BlockSpec((1, tt, nh * dk), lambda b, j: (b, j, k_col)),
                  pl.BlockSpec((1, tt, nh * dv), lambda b, j: (b, j, v_col)),
                  pl.BlockSpec((1, tt, nh * dv), lambda b, j: (b, j, g_col)),
                  pl.BlockSpec((1, tt, LANES), lambda b, j: (b, j, 0)),
                  pl.BlockSpec((LANES, nh * dk), lambda b, j: (0, 0)),
                  pl.BlockSpec((1, nh * dk), lambda b, j: (0, 0)),
                  pl.BlockSpec((1, nh * dv), lambda b, j: (0, 0)),
                  pl.BlockSpec((1, nh, dk, dv), lambda b, j: (b, 0, 0, 0))],
        out_specs=(pl.BlockSpec((1, tt, nh * dv), lambda b, j: (b, j, 0)),
                   pl.BlockSpec((1, nh, dk, dv), lambda b, j: (b, 0, 0, 0))),
        scratch_shapes=[pltpu.VMEM((nh, dk, dv), F32)],
        compiler_params=_cparams(("arbitrary", "arbitrary")),
        name="gla",
    )(proj, proj, proj, proj, lra, w2p, b2, ng, s0)


def _sb_body(*refs, tq, tk, t_new, t_past, hd):
    if t_past:
        q_ref, k_ref, v_ref, kp_ref, vp_ref, o_ref = refs
    else:
        q_ref, k_ref, v_ref, o_ref = refs
    qi = pl.program_id(2)
    lane_q = _iota((1, LANES), 1)
    head0 = lane_q < hd
    q2 = q_ref[0].astype(F32) * (hd ** -0.5)
    q2 = q2.astype(BF16)

    def make_consts(w):
        r = _iota((2 * w, 2 * w), 0)
        c = _iota((2 * w, 2 * w), 1)
        same = (r < w) == (c < w)
        u = jnp.where(same & (r > c), 1.0, 0.0).astype(BF16)
        lane = _iota((1, 2 * w), 1)
        return u, lane < w

    wd = min(tk, tq)
    consts = {w: make_consts(w) for w in {wd, tk}}

    def block(kb, vb, w, carry, mask):
        acc, r0, r1 = carry
        u, first = consts[w]
        kb = kb.astype(BF16)
        vb = vb.astype(BF16)
        zero = jnp.zeros_like(kb)
        k2 = jnp.concatenate([jnp.where(head0, kb, zero), jnp.where(head0, zero, kb)], axis=0)
        v2 = jnp.concatenate([jnp.where(head0, vb, zero), jnp.where(head0, zero, vb)], axis=0)
        z = _dot_nt(q2, k2)
        sp = _softplus(z)
        l1m = -sp
        lb = z - sp
        if mask is not None:
            l1m = jnp.where(mask, l1m, 0.0)
        hi, lo = _split_bf16(l1m)
        rest = _dot(hi, u) + _dot(lo, u)
        rest = rest + jnp.where(first, r0, r1)
        wgt = jnp.exp(lb + rest)
        if mask is not None:
            wgt = jnp.where(mask, wgt, 0.0)
        acc = acc + _dot(wgt.astype(BF16), v2)
        r0 = r0 + jnp.sum(jnp.where(first, l1m, 0.0), axis=1, keepdims=True)
        r1 = r1 + jnp.sum(jnp.where(first, 0.0, l1m), axis=1, keepdims=True)
        return acc, r0, r1

    carry = (jnp.zeros((tq, LANES), F32), jnp.zeros((tq, 1), F32), jnp.zeros((tq, 1), F32))

    n_diag = tq // wd
    qpos = _iota((tq, 2 * wd), 0)
    kcol = _iota((tq, 2 * wd), 1)
    kcol = jnp.where(kcol < wd, kcol, kcol - wd)
    for d in range(n_diag - 1, -1, -1):
        start = pl.multiple_of(qi * tq + d * wd, wd)
        mask = (kcol + d * wd) < qpos
        carry = block(k_ref[0, pl.ds(start, wd), :], v_ref[0, pl.ds(start, wd), :], wd, carry, mask)

    def full(i, c):
        jb = qi * (tq // tk) - 1 - i
        start = pl.multiple_of(jb * tk, tk)
        return block(k_ref[0, pl.ds(start, tk), :], v_ref[0, pl.ds(start, tk), :], tk, c, None)

    if t_new > tq:
        carry = lax.fori_loop(0, qi * (tq // tk), full, carry)

    if t_past:
        def past(i, c):
            start = pl.multiple_of(t_past - tk - i * tk, tk)
            return block(kp_ref[0, pl.ds(start, tk), :], vp_ref[0, pl.ds(start, tk), :], tk, c, None)
        carry = lax.fori_loop(0, t_past // tk, past, carry)

    o_ref[0] = carry[0].astype(o_ref.dtype)


def _sb(proj, q_col, k_new, v_new, k_past=None, v_past=None):
    bsz, t, _ = proj.shape
    width = k_new.shape[-1]
    hd = width // SB_HEADS
    n_pairs = width // LANES
    tq = min(t, 256)
    tk = 128
    t_past = 0 if k_past is None else k_past.shape[1]
    body = functools.partial(_sb_body, tq=tq, tk=tk, t_new=t, t_past=t_past, hd=hd)
    in_specs = [pl.BlockSpec((1, tq, LANES), lambda b, p, i: (b, i, q_col + p)),
                pl.BlockSpec((1, t, LANES), lambda b, p, i: (b, 0, p)),
                pl.BlockSpec((1, t, LANES), lambda b, p, i: (b, 0, p))]
    args = [proj, k_new, v_new]
    if t_past:
        in_specs += [pl.BlockSpec((1, t_past, LANES), lambda b, p, i: (b, 0, p)),
                     pl.BlockSpec((1, t_past, LANES), lambda b, p, i: (b, 0, p))]
        args += [k_past, v_past]
    return pl.pallas_call(
        body,
        out_shape=jax.ShapeDtypeStruct((bsz, t, width), BF16),
        grid=(bsz, n_pairs, t // tq),
        in_specs=in_specs,
        out_specs=pl.BlockSpec((1, tq, LANES), lambda b, p, i: (b, i, p)),
        compiler_params=_cparams(("arbitrary", "arbitrary", "arbitrary")),
        name="stickbreak",
    )(*args)


def _merge_body(oa_ref, ob_ref, ga_ref, gb_ref, x_ref, wa_ref, wb_ref, wo_ref,
                gm_ref, sf_ref, cf_ref, l1g_ref, l1b_ref, x1_ref, h2_ref, *, alpha):
    tb, tt, d = x_ref.shape
    n = tb * tt
    a = _dot(oa_ref[...].reshape(n, d), wa_ref[...])
    b = _dot(ob_ref[...].reshape(n, d), wb_ref[...])
    ga = _sigmoid(ga_ref[...].reshape(n, d).astype(F32))
    gb = _sigmoid(gb_ref[...].reshape(n, d).astype(F32))
    merged = (ga * a + gb * b).astype(BF16)
    mix = _dot(merged, wo_ref[...]).reshape(tb, tt, d)
    x = x_ref[...]
    x1 = _ln(alpha * x + gm_ref[...] * mix) * l1g_ref[...] + l1b_ref[...]
    x1_ref[...] = x1
    h2 = _ln(x1) * (1.0 + cf_ref[...]) + sf_ref[...]
    h2_ref[...] = h2.astype(BF16).astype(F32)


def _merge(oa, ob, proj, ga_col, gb_col, x, wa, wb, wo, gate_m, shift_f, scale_f, l1g, l1b, alpha):
    bsz, t, d = x.shape
    tb, tt = _row_tiles(bsz, t, 512)
    tok = lambda i, j: (i, j, 0)
    per_b = lambda i, j: (i, 0, 0)
    const2 = lambda i, j: (0, 0)
    const3 = lambda i, j: (0, 0, 0)
    body = functools.partial(_merge_body, alpha=alpha)
    return pl.pallas_call(
        body,
        out_shape=(jax.ShapeDtypeStruct((bsz, t, d), F32), jax.ShapeDtypeStruct((bsz, t, d), F32)),
        grid=(bsz // tb, t // tt),
        in_specs=[pl.BlockSpec((tb, tt, d), tok),
                  pl.BlockSpec((tb, tt, d), tok),
                  pl.BlockSpec((tb, tt, d), lambda i, j: (i, j, ga_col)),
                  pl.BlockSpec((tb, tt, d), lambda i, j: (i, j, gb_col)),
                  pl.BlockSpec((tb, tt, d), tok),
                  pl.BlockSpec((d, d), const2), pl.BlockSpec((d, d), const2), pl.BlockSpec((d, d), const2),
                  pl.BlockSpec((tb, 1, d), per_b), pl.BlockSpec((tb, 1, d), per_b),
                  pl.BlockSpec((tb, 1, d), per_b),
                  pl.BlockSpec((1, 1, d), const3), pl.BlockSpec((1, 1, d), const3)],
        out_specs=(pl.BlockSpec((tb, tt, d), tok), pl.BlockSpec((tb, tt, d), tok)),
        compiler_params=_cparams(("arbitrary", "arbitrary")),
        name="merge",
    )(oa, ob, proj, proj, x, wa, wb, wo,
      gate_m.reshape(bsz, 1, d), shift_f.reshape(bsz, 1, d), scale_f.reshape(bsz, 1, d),
      l1g.reshape(1, 1, d), l1b.reshape(1, 1, d))


def _router_body(h_ref, w_ref, b_ref, route_ref, cnt_ref, run_ref, *, tm):
    i = pl.program_id(0)

    @pl.when(i == 0)
    def _():
        run_ref[...] = jnp.zeros_like(run_ref)

    logits = _dot(h_ref[...].astype(BF16), w_ref[...]) + b_ref[...]
    lane = _iota((tm, LANES), 1).astype(F32)
    is_g = lane < N_GROUPS
    is_e = (lane >= N_GROUPS) & (lane < N_GROUPS + N_EXPERTS)

    def amax(mask):
        m = jnp.max(jnp.where(mask, logits, NEG_BIG), axis=1, keepdims=True)
        idx = jnp.min(jnp.where(mask & (logits == m), lane, float(LANES)), axis=1, keepdims=True)
        return m, idx

    gm, gi = amax(is_g)
    p_top = 1.0 / jnp.sum(jnp.where(is_g, jnp.exp(logits - gm), 0.0), axis=1, keepdims=True)
    lo = N_GROUPS + gi * EXPERTS_PER_GROUP
    in_grp = is_e & (lane >= lo) & (lane < lo + EXPERTS_PER_GROUP)
    v1, i1 = amax(in_grp)
    v2, i2 = amax(in_grp & (lane != i1))
    e21 = jnp.exp(v2 - v1)
    w1 = p_top / (1.0 + e21)
    w2 = p_top * e21 / (1.0 + e21)
    e1 = i1 - N_GROUPS
    e2 = i2 - N_GROUPS

    oh = jnp.where((lane == e1) | (lane == e2), 1.0, 0.0).astype(BF16)
    tri = jnp.where(_iota((tm, tm), 0) >= _iota((tm, tm), 1), 1.0, 0.0).astype(BF16)
    cnt = _dot(tri, oh) + run_ref[...]
    rank1 = jnp.sum(jnp.where(lane == e1, cnt, 0.0), axis=1, keepdims=True) - 1.0
    rank2 = jnp.sum(jnp.where(lane == e2, cnt, 0.0), axis=1, keepdims=True) - 1.0
    run_ref[...] = cnt[tm - 1:tm, :]
    cnt_ref[...] = cnt[tm - 1:tm, :]

    out = jnp.where(lane == 0, e1, 0.0)
    out = jnp.where(lane == 1, e2, out)
    out = jnp.where(lane == 2, w1, out)
    out = jnp.where(lane == 3, w2, out)
    out = jnp.where(lane == 4, rank1, out)
    out = jnp.where(lane == 5, rank2, out)
    route_ref[...] = out


def _router(h2, w_rt, b_rt):
    n, d = h2.shape
    tm = min(n, 512)
    body = functools.partial(_router_body, tm=tm)
    return pl.pallas_call(
        body,
        out_shape=(jax.ShapeDtypeStruct((n, LANES), F32), jax.ShapeDtypeStruct((1, LANES), F32)),
        grid=(n // tm,),
        in_specs=[pl.BlockSpec((tm, d), lambda i: (i, 0)),
                  pl.BlockSpec((d, LANES), lambda i: (0, 0)),
                  pl.BlockSpec((1, LANES), lambda i: (0, 0))],
        out_specs=(pl.BlockSpec((tm, LANES), lambda i: (i, 0)), pl.BlockSpec((1, LANES), lambda i: (0, 0))),
        scratch_shapes=[pltpu.VMEM((1, LANES), F32)],
        compiler_params=_cparams(("arbitrary",)),
        name="router",
    )(h2, w_rt, b_rt)


def _dispatch_body(pos_ref, h_ref, hs_ref, sem, *, tm):
    i = pl.program_id(0)
    base = i * tm

    def copy(t, slot):
        dst = pos_ref[2 * (base + t) + slot]
        return pltpu.make_async_copy(h_ref.at[pl.ds(t, 1)], hs_ref.at[pl.ds(dst, 1)], sem)

    def start(t, c):
        copy(t, 0).start()
        copy(t, 1).start()
        return c

    def wait(t, c):
        copy(t, 0).wait()
        copy(t, 1).wait()
        return c

    lax.fori_loop(0, tm, start, 0)
    lax.fori_loop(0, tm, wait, 0)


def _dispatch(pos, h2, n_rows):
    n, d = h2.shape
    tm = min(n, 256)
    body = functools.partial(_dispatch_body, tm=tm)
    grid_spec = pltpu.PrefetchScalarGridSpec(
        num_scalar_prefetch=1,
        grid=(n // tm,),
        in_specs=[pl.BlockSpec((tm, d), lambda i, pos: (i, 0))],
        out_specs=pl.BlockSpec(memory_space=pl.ANY),
        scratch_shapes=[pltpu.SemaphoreType.DMA],
    )
    return pl.pallas_call(
        body,
        out_shape=jax.ShapeDtypeStruct((n_rows, d), h2.dtype),
        grid_spec=grid_spec,
        compiler_params=pltpu.CompilerParams(dimension_semantics=("arbitrary",),
                                             vmem_limit_bytes=VMEM_LIMIT, has_side_effects=True),
        name="dispatch",
    )(pos, h2)


def _ffn_body(meta_ref, x_ref, wg_ref, wu_ref, wd_ref, o_ref, *, tm, n_vis):
    v = pl.program_id(0)
    nv = meta_ref[0]

    @pl.when(v < nv)
    def _():
        tile = meta_ref[1 + v]
        e = meta_ref[1 + n_vis + v]
        first = meta_ref[1 + 2 * n_vis + v]
        lo = meta_ref[1 + 3 * n_vis + e]
        hi = meta_ref[2 + 3 * n_vis + e]
        x = x_ref[...].astype(BF16)
        g = _dot(x, wg_ref[0])
        u = _dot(x, wu_ref[0])
        he = (_silu(g) * u).astype(BF16)
        y = _dot(he, wd_ref[0])
        row = tile * tm + _iota((tm, 1), 0)
        mine = (row >= lo) & (row < hi)

        @pl.when(first == 1)
        def _():
            o_ref[...] = jnp.where(mine, y, 0.0)

        @pl.when(first == 0)
        def _():
            o_ref[...] = jnp.where(mine, y, o_ref[...])


def _ffn(meta, hs, wg, wu, wd, n_vis):
    n_rows, d = hs.shape
    ff = wg.shape[-1]
    tm = FFN_TILE
    body = functools.partial(_ffn_body, tm=tm, n_vis=n_vis)
    tile_of = lambda v, m: (m[1 + jnp.minimum(v, m[0] - 1)], 0)
    exp_of = lambda v, m: (m[1 + n_vis + jnp.minimum(v, m[0] - 1)], 0, 0)
    grid_spec = pltpu.PrefetchScalarGridSpec(
        num_scalar_prefetch=1,
        grid=(n_vis,),
        in_specs=[pl.BlockSpec((tm, d), tile_of),
                  pl.BlockSpec((1, d, ff), exp_of),
                  pl.BlockSpec((1, d, ff), exp_of),
                  pl.BlockSpec((1, ff, d), exp_of)],
        out_specs=pl.BlockSpec((tm, d), tile_of),
    )
    return pl.pallas_call(
        body,
        out_shape=jax.ShapeDtypeStruct((n_rows, d), F32),
        grid_spec=grid_spec,
        compiler_params=_cparams(("arbitrary",)),
        name="expert_ffn",
    )(meta, hs, wg, wu, wd)


def _combine_body(pos_ref, ys_ref, route_ref, x1_ref, gf_ref, g_ref, b_ref, o_ref, buf, sem,
                  *, tm, alpha):
    i = pl.program_id(0)
    base = i * tm

    def copy(t, slot):
        src = pos_ref[2 * (base + t) + slot]
        return pltpu.make_async_copy(ys_ref.at[pl.ds(src, 1)], buf.at[slot, pl.ds(t, 1)], sem)

    def start(t, c):
        copy(t, 0).start()
        copy(t, 1).start()
        return c

    def wait(t, c):
        copy(t, 0).wait()
        copy(t, 1).wait()
        return c

    lax.fori_loop(0, tm, start, 0)
    lax.fori_loop(0, tm, wait, 0)
    tb, tt, d = x1_ref.shape
    route = route_ref[...]
    w1 = route[:, 2:3]
    w2 = route[:, 3:4]
    moe = (w1 * buf[0] + w2 * buf[1]).reshape(tb, tt, d)
    y = _ln(alpha * x1_ref[...] + gf_ref[...] * moe) * g_ref[...] + b_ref[...]
    o_ref[...] = y


def _combine(pos, ys, route, x1, gate_f, l2g, l2b, alpha):
    bsz, t, d = x1.shape
    tb, tt = _row_tiles(bsz, t, 256)
    tm = tb * tt
    nj = t // tt
    body = functools.partial(_combine_body, tm=tm, alpha=alpha)
    grid_spec = pltpu.PrefetchScalarGridSpec(
        num_scalar_prefetch=1,
        grid=(bsz * t // tm,),
        in_specs=[pl.BlockSpec(memory_space=pl.ANY),
                  pl.BlockSpec((tm, LANES), lambda i, pos: (i, 0)),
                  pl.BlockSpec((tb, tt, d), lambda i, pos: (i // nj, i % nj, 0)),
                  pl.BlockSpec((tb, 1, d), lambda i, pos: (i // nj, 0, 0)),
                  pl.BlockSpec((1, 1, d), lambda i, pos: (0, 0, 0)),
                  pl.BlockSpec((1, 1, d), lambda i, pos: (0, 0, 0))],
        out_specs=pl.BlockSpec((tb, tt, d), lambda i, pos: (i // nj, i % nj, 0)),
        scratch_shapes=[pltpu.VMEM((2, tm, d), F32), pltpu.SemaphoreType.DMA],
    )
    return pl.pallas_call(
        body,
        out_shape=jax.ShapeDtypeStruct((bsz, t, d), F32),
        grid_spec=grid_spec,
        compiler_params=_cparams(("arbitrary",)),
        name="combine",
    )(pos, ys, route, x1, gate_f.reshape(bsz, 1, d), l2g.reshape(1, 1, d), l2b.reshape(1, 1, d))


def _routing_tables(route, counts, n_tokens, n_vis):
    tm = FFN_TILE
    cnt = counts[0, :N_EXPERTS].astype(I32)
    ends = jnp.cumsum(cnt)
    offs = jnp.concatenate([jnp.zeros((1,), I32), ends])
    eid = route[:, 0:2].astype(I32)
    rank = route[:, 4:6].astype(I32)
    onehot = eid[:, :, None] == jnp.arange(N_EXPERTS, dtype=I32)[None, None, :]
    pos = rank + jnp.sum(jnp.where(onehot, offs[None, None, :N_EXPERTS], 0), axis=-1)
    pos = pos.reshape(-1)

    n_tiles = (2 * n_tokens) // tm
    t_lo = jnp.arange(n_tiles, dtype=I32)[:, None] * tm
    inter = (offs[None, :-1] < t_lo + tm) & (offs[None, 1:] > t_lo) & (cnt[None, :] > 0)
    flat = inter.reshape(-1)
    order = jnp.argsort(jnp.where(flat, 0, 1), stable=True)[:n_vis].astype(I32)
    nv = jnp.sum(flat.astype(I32))
    v_tile = order // N_EXPERTS
    v_exp = order % N_EXPERTS
    valid = jnp.arange(n_vis, dtype=I32) < nv
    v_tile = jnp.where(valid, v_tile, 0)
    v_exp = jnp.where(valid, v_exp, 0)
    prev_tile = jnp.concatenate([jnp.full((1,), -1, I32), v_tile[:-1]])
    v_first = (v_tile != prev_tile).astype(I32)
    meta = jnp.concatenate([nv[None], v_tile, v_exp, v_first, offs]).astype(I32)
    return pos, meta


def _layer(x, mod, wts, sb_past, s0):
    bsz, t, d = x.shape
    n = bsz * t
    shift_m, scale_m, gate_m, shift_f, scale_f, gate_f = [mod[:, i * d:(i + 1) * d] for i in range(6)]
    alpha = wts["alpha"]

    h = _lnmod(x, scale_m, shift_m).reshape(n, d)
    proj = _matmul(h, wts["w_main"], BF16, "proj_main").reshape(bsz, t, -1)
    kb = _matmul(h, wts["w_kb"], F32, "proj_kb")
    vb = _matmul(h, wts["w_vb"], F32, "proj_vb")
    lra = _matmul(h, wts["w_lr"], F32, "proj_lr").reshape(bsz, t, LANES)

    oa, s_new = _gla(proj, lra, wts["w_a2"], wts["b_a2"], wts["norm_g"], s0,
                     q_col=0, k_col=1, v_col=1, g_col=2)
    kb3 = kb.reshape(bsz, t, d)
    vb3 = vb.reshape(bsz, t, d)
    if sb_past is None:
        ob = _sb(proj, 3072 // LANES, kb3, vb3)
    else:
        ob = _sb(proj, 3072 // LANES, kb3, vb3, sb_past[0], sb_past[1])

    x1, h2 = _merge(oa, ob, proj, 4, 5, x, wts["w_ba"], wts["w_bb"], wts["w_out"],
                    gate_m, shift_f, scale_f, wts["ln1_g"], wts["ln1_b"], alpha)
    h2 = h2.reshape(n, d)

    route, counts = _router(h2, wts["w_rt"], wts["b_rt"])
    n_vis = (2 * n) // FFN_TILE + N_EXPERTS - 1
    pos, meta = _routing_tables(route, counts, n, n_vis)
    hs = _dispatch(pos, h2, 2 * n)
    ys = _ffn(meta, hs, wts["w_eg"], wts["w_eu"], wts["w_ed"], n_vis)
    y = _combine(pos, ys, route, x1, gate_f, wts["ln2_g"], wts["ln2_b"], alpha)
    return y, kb, vb, s_new


def kernel(x_prompt, x_sample, cache_sb_k, cache_sb_v, state_gla, c_prompt, c_sample, w_cond, b_cond, w_in, w_gla_a2, b_gla_a, gla_norm_g, w_branch_a, w_branch_b, w_out, ln1_g, ln1_b, w_group, b_group, w_router, b_router, w_exp_gate, w_exp_up, w_exp_down, ln2_g, ln2_b):
    depth = w_in.shape[0]
    d = x_prompt.shape[-1]
    bp, tp, _ = x_prompt.shape
    bs, ts, _ = x_sample.shape
    nh, dk, dv = state_gla.shape[2], state_gla.shape[3], state_gla.shape[4]
    qk = nh * dk
    gv = nh * dv
    sbh, sbd = cache_sb_k.shape[3], cache_sb_k.shape[4]
    sbw = sbh * sbd
    alpha = (2.0 * depth) ** 0.25

    yp, ys = x_prompt, x_sample
    outs = [[] for _ in range(6)]
    for l in range(depth):
        wi = w_in[l]
        o = [0]
        for wdt in (qk, qk, gv, gv, GLA_LOW_RANK, sbw, sbw, sbw, d, d):
            o.append(o[-1] + wdt)
        col = lambda i: wi[:, o[i]:o[i + 1]]
        w_main = jnp.concatenate([col(0), col(1), col(2), col(3), col(5), col(8), col(9)], axis=1).astype(BF16)
        w_lr = jnp.pad(col(4), ((0, 0), (0, LANES - GLA_LOW_RANK))).astype(BF16)
        w_a2 = jnp.pad(w_gla_a2[l], ((0, LANES - GLA_LOW_RANK), (0, 0))).astype(BF16)
        w_rt = jnp.pad(jnp.concatenate([w_group[l], w_router[l]], axis=1),
                       ((0, 0), (0, LANES - N_GROUPS - N_EXPERTS))).astype(BF16)
        b_rt = jnp.pad(jnp.concatenate([b_group[l], b_router[l]]), (0, LANES - N_GROUPS - N_EXPERTS)).reshape(1, LANES)
        wts = dict(
            alpha=alpha, w_main=w_main, w_kb=col(6).astype(BF16), w_vb=col(7).astype(BF16), w_lr=w_lr,
            w_a2=w_a2, b_a2=b_gla_a[l].reshape(1, qk), norm_g=gla_norm_g[l].reshape(1, gv),
            w_ba=w_branch_a[l].astype(BF16), w_bb=w_branch_b[l].astype(BF16), w_out=w_out[l].astype(BF16),
            ln1_g=ln1_g[l], ln1_b=ln1_b[l], w_rt=w_rt, b_rt=b_rt,
            w_eg=w_exp_gate[l].astype(BF16), w_eu=w_exp_up[l].astype(BF16), w_ed=w_exp_down[l].astype(BF16),
            ln2_g=ln2_g[l], ln2_b=ln2_b[l])
        mod = _cond(jnp.concatenate([c_prompt, c_sample], axis=0), w_cond[l].astype(BF16), b_cond[l])
        s0p = jnp.zeros((bp, nh, dk, dv), F32)
        yp, kp, vp, sp = _layer(yp, mod[:bp], wts, None, s0p)
        past = (cache_sb_k[l].reshape(bs, -1, sbw), cache_sb_v[l].reshape(bs, -1, sbw))
        ys, ks, vs, ss = _layer(ys, mod[bp:], wts, past, state_gla[l])
        for lst, val in zip(outs, (kp.reshape(bp, tp, sbh, sbd), vp.reshape(bp, tp, sbh, sbd), sp,
                                   ks.reshape(bs, ts, sbh, sbd), vs.reshape(bs, ts, sbh, sbd), ss)):
            lst.append(val)
    return (yp, ys) + tuple(jnp.stack(v) for v in outs)
```

```python
import functools
import math

import jax
import jax.numpy as jnp
from jax import lax
from jax.experimental import pallas as pl
from jax.experimental.pallas import tpu as pltpu

F32 = jnp.float32
BF16 = jnp.bfloat16
I32 = jnp.int32

LN_EPS = 1e-5
GLA_HEADS = 4
GLA_CHUNK = 64
GLA_TAU = 16.0
GLA_LOW_RANK = 16
SB_HEADS = 16
N_GROUPS = 4
EXPERTS_PER_GROUP = 8
N_EXPERTS = N_GROUPS * EXPERTS_PER_GROUP
LANES = 128
VMEM_LIMIT = 56 * 1024 * 1024
FFN_TILE = 256
NEG_BIG = -1e30


def _cparams(sem, vmem=VMEM_LIMIT):
    return pltpu.CompilerParams(dimension_semantics=sem, vmem_limit_bytes=vmem)


def _iota(shape, dim):
    return lax.broadcasted_iota(I32, shape, dim)


def _sigmoid(x):
    return 1.0 / (1.0 + jnp.exp(-x))


def _silu(x):
    return x * _sigmoid(x)


def _softplus(x):
    return jnp.maximum(x, 0.0) + jnp.log1p(jnp.exp(-jnp.abs(x)))


def _ln(x):
    mu = jnp.mean(x, axis=-1, keepdims=True)
    xc = x - mu
    var = jnp.mean(xc * xc, axis=-1, keepdims=True)
    return xc * lax.rsqrt(var + LN_EPS)


def _split_bf16(x):
    hi = x.astype(BF16)
    lo = (x - hi.astype(F32)).astype(BF16)
    return hi, lo


def _dot(a, b):
    return jnp.dot(a, b, preferred_element_type=F32)


def _dot_nt(a, b):
    return lax.dot_general(a, b, (((1,), (1,)), ((), ())), preferred_element_type=F32)


def _dot_tn(a, b):
    return lax.dot_general(a, b, (((0,), (0,)), ((), ())), preferred_element_type=F32)


def _cond_body(c_ref, w_ref, b_ref, o_ref):
    c = _silu(c_ref[...]).astype(BF16)
    o_ref[...] = _dot(c, w_ref[...]) + b_ref[...]


def _cond(c, w_bf, b):
    rows, d = c.shape
    n = w_bf.shape[1]
    tn = 1024
    return pl.pallas_call(
        _cond_body,
        out_shape=jax.ShapeDtypeStruct((rows, n), F32),
        grid=(n // tn,),
        in_specs=[pl.BlockSpec((rows, d), lambda j: (0, 0)),
                  pl.BlockSpec((d, tn), lambda j: (0, j)),
                  pl.BlockSpec((1, tn), lambda j: (0, j))],
        out_specs=pl.BlockSpec((rows, tn), lambda j: (0, j)),
        compiler_params=_cparams(("arbitrary",)),
        name="cond",
    )(c, w_bf, b.reshape(1, n))


def _lnmod_body(x_ref, sc_ref, sh_ref, o_ref):
    x = x_ref[...]
    h = _ln(x) * (1.0 + sc_ref[...]) + sh_ref[...]
    o_ref[...] = h.astype(o_ref.dtype)


def _row_tiles(bsz, t, target):
    if t >= target:
        return 1, target
    return min(bsz, target // t), t


def _lnmod(x, scale, shift):
    bsz, t, d = x.shape
    tb, tt = _row_tiles(bsz, t, 512)
    return pl.pallas_call(
        _lnmod_body,
        out_shape=jax.ShapeDtypeStruct((bsz, t, d), BF16),
        grid=(bsz // tb, t // tt),
        in_specs=[pl.BlockSpec((tb, tt, d), lambda i, j: (i, j, 0)),
                  pl.BlockSpec((tb, 1, d), lambda i, j: (i, 0, 0)),
                  pl.BlockSpec((tb, 1, d), lambda i, j: (i, 0, 0))],
        out_specs=pl.BlockSpec((tb, tt, d), lambda i, j: (i, j, 0)),
        compiler_params=_cparams(("arbitrary", "arbitrary")),
        name="lnmod",
    )(x, scale.reshape(bsz, 1, d), shift.reshape(bsz, 1, d))


def _mm_body(x_ref, w_ref, o_ref):
    o_ref[...] = _dot(x_ref[...], w_ref[...]).astype(o_ref.dtype)


def _matmul(x, w, out_dtype, name):
    n, k = x.shape
    m = w.shape[1]
    tm = min(n, 1024)
    tn = 2048 if m % 2048 == 0 else (1024 if m % 1024 == 0 else m)
    return pl.pallas_call(
        _mm_body,
        out_shape=jax.ShapeDtypeStruct((n, m), out_dtype),
        grid=(n // tm, m // tn),
        in_specs=[pl.BlockSpec((tm, k), lambda i, j: (i, 0)),
                  pl.BlockSpec((k, tn), lambda i, j: (0, j))],
        out_specs=pl.BlockSpec((tm, tn), lambda i, j: (i, j)),
        compiler_params=_cparams(("arbitrary", "arbitrary")),
        name=name,
    )(x, w)


def _gla_body(q_ref, k_ref, v_ref, g_ref, lr_ref, w2_ref, b2_ref, ng_ref, s0_ref,
              o_ref, sout_ref, st_ref, *, n_chunks, dk, dv):
    j = pl.program_id(1)
    nh = GLA_HEADS
    c = GLA_CHUNK

    @pl.when(j == 0)
    def _():
        st_ref[...] = s0_ref[0]

    row = _iota((c, c), 0)
    col = _iota((c, c), 1)
    tril = (row >= col)
    tril_bf = jnp.where(tril, 1.0, 0.0).astype(BF16)
    eye_k = _iota((dk, dk), 0) == _iota((dk, dk), 1)

    def chunk(ci, carry):
        r0 = pl.multiple_of(ci * c, c)
        sl = pl.ds(r0, c)
        lr = lr_ref[0, sl, :].astype(BF16)
        pre = _dot(lr, w2_ref[...]) + b2_ref[...]
        la = -_softplus(-pre) * (1.0 / GLA_TAU)
        hi, lo = _split_bf16(la)
        bc = _dot(tril_bf, hi) + _dot(tril_bf, lo)
        b_last = bc[c - 1:c, :]
        q = q_ref[0, sl, :].astype(F32) * (dk ** -0.5)
        k = k_ref[0, sl, :].astype(F32)
        qe = (q * jnp.exp(bc)).astype(BF16)
        ke = (k * jnp.exp(-bc)).astype(BF16)
        kd = (k * jnp.exp(b_last - bc)).astype(BF16)
        e_last = jnp.exp(b_last)
        for h in range(nh):
            ks = slice(h * dk, (h + 1) * dk)
            vs = slice(h * dv, (h + 1) * dv)
            v_h = v_ref[0, sl, vs]
            s_h = st_ref[h]
            o_inter = _dot(qe[:, ks], s_h.astype(BF16))
            sc = _dot_nt(qe[:, ks], ke[:, ks])
            sc = jnp.where(tril, sc, 0.0).astype(BF16)
            o = o_inter + _dot(sc, v_h)
            e_col = jnp.sum(jnp.where(eye_k, jnp.broadcast_to(e_last[:, ks], (dk, dk)), 0.0),
                            axis=1, keepdims=True)
            st_ref[h] = e_col * s_h + _dot_tn(kd[:, ks], v_h)
            ms = jnp.mean(o * o, axis=-1, keepdims=True)
            on = o * lax.rsqrt(ms + LN_EPS) * ng_ref[:, vs]
            gate = g_ref[0, sl, vs].astype(F32)
            o_ref[0, sl, vs] = (on * _silu(gate)).astype(o_ref.dtype)
        return carry

    lax.fori_loop(0, n_chunks, chunk, 0)

    @pl.when(j == pl.num_programs(1) - 1)
    def _():
        sout_ref[0] = st_ref[...]


def _gla(proj, lra, w2p, b2, ng, s0, *, q_col, k_col, v_col, g_col):
    bsz, t, _ = proj.shape
    nh, dk, dv = s0.shape[1], s0.shape[2], s0.shape[3]
    tt = min(t, 256)
    n_chunks = tt // GLA_CHUNK
    body = functools.partial(_gla_body, n_chunks=n_chunks, dk=dk, dv=dv)
    return pl.pallas_call(
        body,
        out_shape=(jax.ShapeDtypeStruct((bsz, t, nh * dv), BF16),
                   jax.ShapeDtypeStruct((bsz, nh, dk, dv), F32)),
        grid=(bsz, t // tt),
        in_specs=[pl.BlockSpec((1, tt, nh * dk), lambda b, j: (b, j, q_col)),
                  pl.BlockSpec((1, tt, nh * dk), lambda b, j: (b, j, k_col)),
                  pl.BlockSpec((1, tt, nh * dv), lambda b, j: (b, j, v_col)),
                  pl.BlockSpec((1, tt, nh * dv), lambda b, j: (b, j, g_col)),
                  pl.BlockSpec((1, tt, LANES), lambda b, j: (b, j, 0)),
                  pl.BlockSpec((LANES, nh * dk), lambda b, j: (0, 0)),
                  pl.BlockSpec((1, nh * dk), lambda b, j: (0, 0)),
                  pl.BlockSpec((1, nh * dv), lambda b, j: (0, 0)),
                  pl.BlockSpec((1, nh, dk, dv), lambda b, j: (b, 0, 0, 0))],
        out_specs=(pl.BlockSpec((1, tt, nh * dv), lambda b, j: (b, j, 0)),
                   pl.BlockSpec((1, nh, dk, dv), lambda b, j: (b, 0, 0, 0))),
        scratch_shapes=[pltpu.VMEM((nh, dk, dv), F32)],
        compiler_params=_cparams(("arbitrary", "arbitrary")),
        name="gla",
    )(proj, proj, proj, proj, lra, w2p, b2, ng, s0)


def _sb_body(*refs, tq, tk, t_new, t_past, hd):
    if t_past:
        (q_ref, k_ref, v_ref, kp_ref, vp_ref, o_ref,
         nz_ref, wg_ref, r_ref, acc_ref, k2n_ref, v2n_ref, k2p_ref, v2p_ref) = refs
    else:
        q_ref, k_ref, v_ref, o_ref, nz_ref, wg_ref, r_ref, acc_ref, k2n_ref, v2n_ref = refs
    qi = pl.program_id(2)
    wd = min(tk, tq)
    n_new = t_new // wd
    n_past = t_past // tk
    head0 = _iota((1, LANES), 1) < hd

    def expand(x):
        x = x.astype(BF16)
        zero = jnp.zeros_like(x)
        return jnp.concatenate([jnp.where(head0, x, zero), jnp.where(head0, zero, x)], axis=0)

    @pl.when(qi == 0)
    def _():
        def fill_new(j, c):
            rows = pl.ds(pl.multiple_of(j * wd, wd), wd)
            k2n_ref[j] = expand(k_ref[0, rows, :])
            v2n_ref[j] = expand(v_ref[0, rows, :])
            return c
        lax.fori_loop(0, n_new, fill_new, 0)
        if t_past:
            def fill_past(j, c):
                rows = pl.ds(pl.multiple_of(j * tk, tk), tk)
                k2p_ref[j] = expand(kp_ref[0, rows, :])
                v2p_ref[j] = expand(vp_ref[0, rows, :])
                return c
            lax.fori_loop(0, n_past, fill_past, 0)

    qn = (q_ref[0].astype(F32) * (-(hd ** -0.5))).astype(BF16)

    def suffix_matrix(w):
        r = _iota((2 * w, 2 * w), 0)
        c = _iota((2 * w, 2 * w), 1)
        same = (r < w) == (c < w)
        return jnp.where(same & (r >= c), 1.0, 0.0).astype(BF16)

    uu = {w: suffix_matrix(w) for w in {wd, tk}}
    first = {w: _iota((1, 2 * w), 1) < w for w in {wd, tk}}

    def head_sums(x, w):
        if w % LANES == 0:
            return (jnp.sum(x[:, :w], axis=1, keepdims=True), jnp.sum(x[:, w:], axis=1, keepdims=True))
        return (jnp.sum(jnp.where(first[w], x, 0.0), axis=1, keepdims=True),
                jnp.sum(jnp.where(first[w], 0.0, x), axis=1, keepdims=True))

    def both_heads(r0, r1, w):
        if w % LANES == 0:
            reps = w // LANES
            return jnp.concatenate([r0] * reps + [r1] * reps, axis=1)
        return jnp.where(first[w], r0[:, :2 * w], r1[:, :2 * w])

    def scores(k2cat):
        return _dot_nt(qn, k2cat)

    def weights(slot, width, w, masks):
        n = width // (2 * w)
        staged = []
        for b in range(n - 1, -1, -1):
            x = nz_ref[slot, :, b * 2 * w:(b + 1) * 2 * w]
            l1m = jnp.minimum(x, 0.0) - jnp.log(1.0 + jnp.exp(-jnp.abs(x)))
            if masks is not None:
                l1m = jnp.where(masks[b], l1m, 0.0)
            incl = _dot(l1m.astype(BF16), uu[w])
            staged.append((b, incl - x, None, head_sums(l1m, w)))
        r0 = r_ref[0]
        r1 = r_ref[1]
        for b, logw, _, (s0, s1) in staged:
            wgt = jnp.exp(logw + both_heads(r0, r1, w))
            if masks is not None:
                wgt = jnp.where(masks[b], wgt, 0.0)
            wg_ref[slot, :, b * 2 * w:(b + 1) * 2 * w] = wgt.astype(BF16)
            r0 = r0 + s0
            r1 = r1 + s1
        r_ref[0] = r0
        r_ref[1] = r1

    n_diag = tq // wd
    gw = n_diag * 2 * wd

    def new_group(ref, top):
        return ref[pl.ds(top - (n_diag - 1), n_diag)].reshape(gw, LANES)

    qpos = _iota((tq, 2 * wd), 0)
    kcol = _iota((tq, 2 * wd), 1)
    kcol = jnp.where(kcol < wd, kcol, kcol - wd)
    masks = [(kcol + dblk * wd) < qpos for dblk in range(n_diag)]
    top0 = (qi + 1) * n_diag - 1
    r_ref[...] = jnp.zeros_like(r_ref)
    nz_ref[0, :, :gw] = scores(new_group(k2n_ref, top0))

    if t_new > tq:
        def top_of(g):
            return top0 - g * n_diag

        nz_ref[1, :, :gw] = scores(new_group(k2n_ref, jnp.maximum(top_of(1), n_diag - 1)))
        weights(0, gw, wd, masks)
        acc_ref[...] = jnp.zeros_like(acc_ref)

        def older(g, c):
            for cur in (0, 1):
                @pl.when(g % 2 == cur)
                def _():
                    nz_ref[1 - cur, :, :gw] = scores(
                        new_group(k2n_ref, jnp.maximum(top_of(g + 1), n_diag - 1)))
                    acc_ref[...] += _dot(wg_ref[1 - cur, :, :gw], new_group(v2n_ref, top_of(g - 1)))
                    weights(cur, gw, wd, None)
            return c

        lax.fori_loop(1, qi + 1, older, 0)
        acc_ref[...] += _dot(wg_ref[qi % 2, :, :gw], new_group(v2n_ref, top_of(qi)))
    else:
        weights(0, gw, wd, masks)
        acc_ref[...] = _dot(wg_ref[0, :, :gw], new_group(v2n_ref, top0))

    if t_past:
        gp = min(n_past, 8)
        pw = gp * 2 * tk

        def past(i, c):
            lo_blk = n_past - (i + 1) * gp
            nz_ref[0, :, :pw] = scores(k2p_ref[pl.ds(lo_blk, gp)].reshape(pw, LANES))
            weights(0, pw, tk, None)
            acc_ref[...] += _dot(wg_ref[0, :, :pw], v2p_ref[pl.ds(lo_blk, gp)].reshape(pw, LANES))
            return c
        lax.fori_loop(0, n_past // gp, past, 0)

    o_ref[0] = acc_ref[...].astype(o_ref.dtype)


def _sb(proj, q_col, k_new, v_new, k_past=None, v_past=None):
    bsz, t, _ = proj.shape
    width = k_new.shape[-1]
    hd = width // SB_HEADS
    n_pairs = width // LANES
    tq = min(t, 256)
    tk = 128
    t_past = 0 if k_past is None else k_past.shape[1]
    body = functools.partial(_sb_body, tq=tq, tk=tk, t_new=t, t_past=t_past, hd=hd)
    in_specs = [pl.BlockSpec((1, tq, LANES), lambda b, p, i: (b, i, q_col + p)),
                pl.BlockSpec((1, t, LANES), lambda b, p, i: (b, 0, p)),
                pl.BlockSpec((1, t, LANES), lambda b, p, i: (b, 0, p))]
    args = [proj, k_new, v_new]
    wd = min(tk, tq)
    group_w = (tq // wd) * 2 * wd
    if t_past:
        group_w = max(group_w, min(t_past // tk, 8) * 2 * tk)
    scratch = [pltpu.VMEM((2, tq, group_w), F32), pltpu.VMEM((2, tq, group_w), BF16),
               pltpu.VMEM((2, tq, LANES), F32), pltpu.VMEM((tq, LANES), F32),
               pltpu.VMEM((t // wd, 2 * wd, LANES), BF16), pltpu.VMEM((t // wd, 2 * wd, LANES), BF16)]
    if t_past:
        in_specs += [pl.BlockSpec((1, t_past, LANES), lambda b, p, i: (b, 0, p)),
                     pl.BlockSpec((1, t_past, LANES), lambda b, p, i: (b, 0, p))]
        args += [k_past, v_past]
        scratch += [pltpu.VMEM((t_past // tk, 2 * tk, LANES), BF16),
                    pltpu.VMEM((t_past // tk, 2 * tk, LANES), BF16)]
    return pl.pallas_call(
        body,
        out_shape=jax.ShapeDtypeStruct((bsz, t, width), BF16),
        grid=(bsz, n_pairs, t // tq),
        in_specs=in_specs,
        out_specs=pl.BlockSpec((1, tq, LANES), lambda b, p, i: (b, i, p)),
        scratch_shapes=scratch,
        compiler_params=_cparams(("arbitrary", "arbitrary", "arbitrary")),
        name="stickbreak",
    )(*args)


def _merge_body(oa_ref, ob_ref, ga_ref, gb_ref, x_ref, wa_ref, wb_ref, wo_ref,
                gm_ref, sf_ref, cf_ref, l1g_ref, l1b_ref, x1_ref, h2_ref, *, alpha):
    tb, tt, d = x_ref.shape
    n = tb * tt
    a = _dot(oa_ref[...].reshape(n, d), wa_ref[...])
    b = _dot(ob_ref[...].reshape(n, d), wb_ref[...])
    ga = _sigmoid(ga_ref[...].reshape(n, d).astype(F32))
    gb = _sigmoid(gb_ref[...].reshape(n, d).astype(F32))
    merged = (ga * a + gb * b).astype(BF16)
    mix = _dot(merged, wo_ref[...]).reshape(tb, tt, d)
    x = x_ref[...]
    x1 = _ln(alpha * x + gm_ref[...] * mix) * l1g_ref[...] + l1b_ref[...]
    x1_ref[...] = x1
    h2 = _ln(x1) * (1.0 + cf_ref[...]) + sf_ref[...]
    h2_ref[...] = h2.astype(BF16).astype(F32)


def _merge(oa, ob, proj, ga_col, gb_col, x, wa, wb, wo, gate_m, shift_f, scale_f, l1g, l1b, alpha):
    bsz, t, d = x.shape
    tb, tt = _row_tiles(bsz, t, 512)
    tok = lambda i, j: (i, j, 0)
    per_b = lambda i, j: (i, 0, 0)
    const2 = lambda i, j: (0, 0)
    const3 = lambda i, j: (0, 0, 0)
    body = functools.partial(_merge_body, alpha=alpha)
    return pl.pallas_call(
        body,
        out_shape=(jax.ShapeDtypeStruct((bsz, t, d), F32), jax.ShapeDtypeStruct((bsz, t, d), F32)),
        grid=(bsz // tb, t // tt),
        in_specs=[pl.BlockSpec((tb, tt, d), tok),
                  pl.BlockSpec((tb, tt, d), tok),
                  pl.BlockSpec((tb, tt, d), lambda i, j: (i, j, ga_col)),
                  pl.BlockSpec((tb, tt, d), lambda i, j: (i, j, gb_col)),
                  pl.BlockSpec((tb, tt, d), tok),
                  pl.BlockSpec((d, d), const2), pl.BlockSpec((d, d), const2), pl.BlockSpec((d, d), const2),
                  pl.BlockSpec((tb, 1, d), per_b), pl.BlockSpec((tb, 1, d), per_b),
                  pl.BlockSpec((tb, 1, d), per_b),
                  pl.BlockSpec((1, 1, d), const3), pl.BlockSpec((1, 1, d), const3)],
        out_specs=(pl.BlockSpec((tb, tt, d), tok), pl.BlockSpec((tb, tt, d), tok)),
        compiler_params=_cparams(("arbitrary", "arbitrary")),
        name="merge",
    )(oa, ob, proj, proj, x, wa, wb, wo,
      gate_m.reshape(bsz, 1, d), shift_f.reshape(bsz, 1, d), scale_f.reshape(bsz, 1, d),
      l1g.reshape(1, 1, d), l1b.reshape(1, 1, d))


def _router_body(h_ref, w_ref, b_ref, route_ref, cnt_ref, run_ref, *, tm):
    i = pl.program_id(0)

    @pl.when(i == 0)
    def _():
        run_ref[...] = jnp.zeros_like(run_ref)

    logits = _dot(h_ref[...].astype(BF16), w_ref[...]) + b_ref[...]
    lane = _iota((tm, LANES), 1).astype(F32)
    is_g = lane < N_GROUPS
    is_e = (lane >= N_GROUPS) & (lane < N_GROUPS + N_EXPERTS)

    def amax(mask):
        m = jnp.max(jnp.where(mask, logits, NEG_BIG), axis=1, keepdims=True)
        idx = jnp.min(jnp.where(mask & (logits == m), lane, float(LANES)), axis=1, keepdims=True)
        return m, idx

    gm, gi = amax(is_g)
    p_top = 1.0 / jnp.sum(jnp.where(is_g, jnp.exp(logits - gm), 0.0), axis=1, keepdims=True)
    lo = N_GROUPS + gi * EXPERTS_PER_GROUP
    in_grp = is_e & (lane >= lo) & (lane < lo + EXPERTS_PER_GROUP)
    v1, i1 = amax(in_grp)
    v2, i2 = amax(in_grp & (lane != i1))
    e21 = jnp.exp(v2 - v1)
    w1 = p_top / (1.0 + e21)
    w2 = p_top * e21 / (1.0 + e21)
    e1 = i1 - N_GROUPS
    e2 = i2 - N_GROUPS

    oh = jnp.where((lane == e1) | (lane == e2), 1.0, 0.0).astype(BF16)
    tri = jnp.where(_iota((tm, tm), 0) >= _iota((tm, tm), 1), 1.0, 0.0).astype(BF16)
    cnt = _dot(tri, oh) + run_ref[...]
    rank1 = jnp.sum(jnp.where(lane == e1, cnt, 0.0), axis=1, keepdims=True) - 1.0
    rank2 = jnp.sum(jnp.where(lane == e2, cnt, 0.0), axis=1, keepdims=True) - 1.0
    run_ref[...] = cnt[tm - 1:tm, :]
    cnt_ref[...] = cnt[tm - 1:tm, :]

    out = jnp.where(lane == 0, e1, 0.0)
    out = jnp.where(lane == 1, e2, out)
    out = jnp.where(lane == 2, w1, out)
    out = jnp.where(lane == 3, w2, out)
    out = jnp.where(lane == 4, rank1, out)
    out = jnp.where(lane == 5, rank2, out)
    route_ref[...] = out


def _router(h2, w_rt, b_rt):
    n, d = h2.shape
    tm = min(n, 512)
    body = functools.partial(_router_body, tm=tm)
    return pl.pallas_call(
        body,
        out_shape=(jax.ShapeDtypeStruct((n, LANES), F32), jax.ShapeDtypeStruct((1, LANES), F32)),
        grid=(n // tm,),
        in_specs=[pl.BlockSpec((tm, d), lambda i: (i, 0)),
                  pl.BlockSpec((d, LANES), lambda i: (0, 0)),
                  pl.BlockSpec((1, LANES), lambda i: (0, 0))],
        out_specs=(pl.BlockSpec((tm, LANES), lambda i: (i, 0)), pl.BlockSpec((1, LANES), lambda i: (0, 0))),
        scratch_shapes=[pltpu.VMEM((1, LANES), F32)],
        compiler_params=_cparams(("arbitrary",)),
        name="router",
    )(h2, w_rt, b_rt)


def _dispatch_body(pos_ref, h_ref, hs_ref, sem, *, tm):
    i = pl.program_id(0)
    base = i * tm

    def copy(t, slot):
        dst = pos_ref[2 * (base + t) + slot]
        return pltpu.make_async_copy(h_ref.at[pl.ds(t, 1)], hs_ref.at[pl.ds(dst, 1)], sem)

    def start(t, c):
        copy(t, 0).start()
        copy(t, 1).start()
        return c

    def wait(t, c):
        copy(t, 0).wait()
        copy(t, 1).wait()
        return c

    lax.fori_loop(0, tm, start, 0)
    lax.fori_loop(0, tm, wait, 0)


def _dispatch(pos, h2, n_rows):
    n, d = h2.shape
    tm = min(n, 256)
    body = functools.partial(_dispatch_body, tm=tm)
    grid_spec = pltpu.PrefetchScalarGridSpec(
        num_scalar_prefetch=1,
        grid=(n // tm,),
        in_specs=[pl.BlockSpec((tm, d), lambda i, pos: (i, 0))],
        out_specs=pl.BlockSpec(memory_space=pl.ANY),
        scratch_shapes=[pltpu.SemaphoreType.DMA],
    )
    return pl.pallas_call(
        body,
        out_shape=jax.ShapeDtypeStruct((n_rows, d), h2.dtype),
        grid_spec=grid_spec,
        compiler_params=pltpu.CompilerParams(dimension_semantics=("arbitrary",),
                                             vmem_limit_bytes=VMEM_LIMIT, has_side_effects=True),
        name="dispatch",
    )(pos, h2)


def _ffn_body(meta_ref, x_ref, wg_ref, wu_ref, wd_ref, o_ref, *, tm, n_vis):
    v = pl.program_id(0)
    nv = meta_ref[0]

    @pl.when(v < nv)
    def _():
        tile = meta_ref[1 + v]
        e = meta_ref[1 + n_vis + v]
        first = meta_ref[1 + 2 * n_vis + v]
        lo = meta_ref[1 + 3 * n_vis + e]
        hi = meta_ref[2 + 3 * n_vis + e]
        x = x_ref[...].astype(BF16)
        g = _dot(x, wg_ref[0])
        u = _dot(x, wu_ref[0])
        he = (_silu(g) * u).astype(BF16)
        y = _dot(he, wd_ref[0])
        row = tile * tm + _iota((tm, 1), 0)
        mine = (row >= lo) & (row < hi)

        @pl.when(first == 1)
        def _():
            o_ref[...] = jnp.where(mine, y, 0.0)

        @pl.when(first == 0)
        def _():
            o_ref[...] = jnp.where(mine, y, o_ref[...])


def _ffn(meta, hs, wg, wu, wd, n_vis):
    n_rows, d = hs.shape
    ff = wg.shape[-1]
    tm = FFN_TILE
    body = functools.partial(_ffn_body, tm=tm, n_vis=n_vis)
    tile_of = lambda v, m: (m[1 + jnp.minimum(v, m[0] - 1)], 0)
    exp_of = lambda v, m: (m[1 + n_vis + jnp.minimum(v, m[0] - 1)], 0, 0)
    grid_spec = pltpu.PrefetchScalarGridSpec(
        num_scalar_prefetch=1,
        grid=(n_vis,),
        in_specs=[pl.BlockSpec((tm, d), tile_of),
                  pl.BlockSpec((1, d, ff), exp_of),
                  pl.BlockSpec((1, d, ff), exp_of),
                  pl.BlockSpec((1, ff, d), exp_of)],
        out_specs=pl.BlockSpec((tm, d), tile_of),
    )
    return pl.pallas_call(
        body,
        out_shape=jax.ShapeDtypeStruct((n_rows, d), F32),
        grid_spec=grid_spec,
        compiler_params=_cparams(("arbitrary",)),
        name="expert_ffn",
    )(meta, hs, wg, wu, wd)


def _combine_body(pos_ref, ys_ref, route_ref, x1_ref, gf_ref, g_ref, b_ref, o_ref, buf, sem,
                  *, tm, alpha):
    i = pl.program_id(0)
    base = i * tm

    def copy(t, slot):
        src = pos_ref[2 * (base + t) + slot]
        return pltpu.make_async_copy(ys_ref.at[pl.ds(src, 1)], buf.at[slot, pl.ds(t, 1)], sem)

    def start(t, c):
        copy(t, 0).start()
        copy(t, 1).start()
        return c

    def wait(t, c):
        copy(t, 0).wait()
        copy(t, 1).wait()
        return c

    lax.fori_loop(0, tm, start, 0)
    lax.fori_loop(0, tm, wait, 0)
    tb, tt, d = x1_ref.shape
    route = route_ref[...]
    w1 = route[:, 2:3]
    w2 = route[:, 3:4]
    moe = (w1 * buf[0] + w2 * buf[1]).reshape(tb, tt, d)
    y = _ln(alpha * x1_ref[...] + gf_ref[...] * moe) * g_ref[...] + b_ref[...]
    o_ref[...] = y


def _combine(pos, ys, route, x1, gate_f, l2g, l2b, alpha):
    bsz, t, d = x1.shape
    tb, tt = _row_tiles(bsz, t, 256)
    tm = tb * tt
    nj = t // tt
    body = functools.partial(_combine_body, tm=tm, alpha=alpha)
    grid_spec = pltpu.PrefetchScalarGridSpec(
        num_scalar_prefetch=1,
        grid=(bsz * t // tm,),
        in_specs=[pl.BlockSpec(memory_space=pl.ANY),
                  pl.BlockSpec((tm, LANES), lambda i, pos: (i, 0)),
                  pl.BlockSpec((tb, tt, d), lambda i, pos: (i // nj, i % nj, 0)),
                  pl.BlockSpec((tb, 1, d), lambda i, pos: (i // nj, 0, 0)),
                  pl.BlockSpec((1, 1, d), lambda i, pos: (0, 0, 0)),
                  pl.BlockSpec((1, 1, d), lambda i, pos: (0, 0, 0))],
        out_specs=pl.BlockSpec((tb, tt, d), lambda i, pos: (i // nj, i % nj, 0)),
        scratch_shapes=[pltpu.VMEM((2, tm, d), F32), pltpu.SemaphoreType.DMA],
    )
    return pl.pallas_call(
        body,
        out_shape=jax.ShapeDtypeStruct((bsz, t, d), F32),
        grid_spec=grid_spec,
        compiler_params=_cparams(("arbitrary",)),
        name="combine",
    )(pos, ys, route, x1, gate_f.reshape(bsz, 1, d), l2g.reshape(1, 1, d), l2b.reshape(1, 1, d))


def _routing_tables(route, counts, n_tokens, n_vis):
    tm = FFN_TILE
    cnt = counts[0, :N_EXPERTS].astype(I32)
    ends = jnp.cumsum(cnt)
    offs = jnp.concatenate([jnp.zeros((1,), I32), ends])
    eid = route[:, 0:2].astype(I32)
    rank = route[:, 4:6].astype(I32)
    onehot = eid[:, :, None] == jnp.arange(N_EXPERTS, dtype=I32)[None, None, :]
    pos = rank + jnp.sum(jnp.where(onehot, offs[None, None, :N_EXPERTS], 0), axis=-1)
    pos = pos.reshape(-1)

    n_tiles = (2 * n_tokens) // tm
    t_lo = jnp.arange(n_tiles, dtype=I32)[:, None] * tm
    inter = (offs[None, :-1] < t_lo + tm) & (offs[None, 1:] > t_lo) & (cnt[None, :] > 0)
    flat = inter.reshape(-1)
    order = jnp.argsort(jnp.where(flat, 0, 1), stable=True)[:n_vis].astype(I32)
    nv = jnp.sum(flat.astype(I32))
    v_tile = order // N_EXPERTS
    v_exp = order % N_EXPERTS
    valid = jnp.arange(n_vis, dtype=I32) < nv
    v_tile = jnp.where(valid, v_tile, 0)
    v_exp = jnp.where(valid, v_exp, 0)
    prev_tile = jnp.concatenate([jnp.full((1,), -1, I32), v_tile[:-1]])
    v_first = (v_tile != prev_tile).astype(I32)
    meta = jnp.concatenate([nv[None], v_tile, v_exp, v_first, offs]).astype(I32)
    return pos, meta


def _layer(x, mod, wts, sb_past, s0):
    bsz, t, d = x.shape
    n = bsz * t
    shift_m, scale_m, gate_m, shift_f, scale_f, gate_f = [mod[:, i * d:(i + 1) * d] for i in range(6)]
    alpha = wts["alpha"]

    h = _lnmod(x, scale_m, shift_m).reshape(n, d)
    proj = _matmul(h, wts["w_main"], BF16, "proj_main").reshape(bsz, t, -1)
    kb = _matmul(h, wts["w_kb"], F32, "proj_kb")
    vb = _matmul(h, wts["w_vb"], F32, "proj_vb")
    lra = _matmul(h, wts["w_lr"], F32, "proj_lr").reshape(bsz, t, LANES)

    oa, s_new = _gla(proj, lra, wts["w_a2"], wts["b_a2"], wts["norm_g"], s0,
                     q_col=0, k_col=1, v_col=1, g_col=2)
    kb3 = kb.reshape(bsz, t, d)
    vb3 = vb.reshape(bsz, t, d)
    if sb_past is None:
        ob = _sb(proj, 3072 // LANES, kb3, vb3)
    else:
        ob = _sb(proj, 3072 // LANES, kb3, vb3, sb_past[0], sb_past[1])

    x1, h2 = _merge(oa, ob, proj, 4, 5, x, wts["w_ba"], wts["w_bb"], wts["w_out"],
                    gate_m, shift_f, scale_f, wts["ln1_g"], wts["ln1_b"], alpha)
    h2 = h2.reshape(n, d)

    route, counts = _router(h2, wts["w_rt"], wts["b_rt"])
    n_vis = (2 * n) // FFN_TILE + N_EXPERTS - 1
    pos, meta = _routing_tables(route, counts, n, n_vis)
    hs = _dispatch(pos, h2, 2 * n)
    ys = _ffn(meta, hs, wts["w_eg"], wts["w_eu"], wts["w_ed"], n_vis)
    y = _combine(pos, ys, route, x1, gate_f, wts["ln2_g"], wts["ln2_b"], alpha)
    return y, kb, vb, s_new


def kernel(x_prompt, x_sample, cache_sb_k, cache_sb_v, state_gla, c_prompt, c_sample, w_cond, b_cond, w_in, w_gla_a2, b_gla_a, gla_norm_g, w_branch_a, w_branch_b, w_out, ln1_g, ln1_b, w_group, b_group, w_router, b_router, w_exp_gate, w_exp_up, w_exp_down, ln2_g, ln2_b):
    depth = w_in.shape[0]
    d = x_prompt.shape[-1]
    bp, tp, _ = x_prompt.shape
    bs, ts, _ = x_sample.shape
    nh, dk, dv = state_gla.shape[2], state_gla.shape[3], state_gla.shape[4]
    qk = nh * dk
    gv = nh * dv
    sbh, sbd = cache_sb_k.shape[3], cache_sb_k.shape[4]
    sbw = sbh * sbd
    alpha = (2.0 * depth) ** 0.25

    yp, ys = x_prompt, x_sample
    outs = [[] for _ in range(6)]
    for l in range(depth):
        wi = w_in[l]
        o = [0]
        for wdt in (qk, qk, gv, gv, GLA_LOW_RANK, sbw, sbw, sbw, d, d):
            o.append(o[-1] + wdt)
        col = lambda i: wi[:, o[i]:o[i + 1]]
        w_main = jnp.concatenate([col(0), col(1), col(2), col(3), col(5), col(8), col(9)], axis=1).astype(BF16)
        w_lr = jnp.pad(col(4), ((0, 0), (0, LANES - GLA_LOW_RANK))).astype(BF16)
        w_a2 = jnp.pad(w_gla_a2[l], ((0, LANES - GLA_LOW_RANK), (0, 0))).astype(BF16)
        w_rt = jnp.pad(jnp.concatenate([w_group[l], w_router[l]], axis=1),
                       ((0, 0), (0, LANES - N_GROUPS - N_EXPERTS))).astype(BF16)
        b_rt = jnp.pad(jnp.concatenate([b_group[l], b_router[l]]), (0, LANES - N_GROUPS - N_EXPERTS)).reshape(1, LANES)
        wts = dict(
            alpha=alpha, w_main=w_main, w_kb=col(6).astype(BF16), w_vb=col(7).astype(BF16), w_lr=w_lr,
            w_a2=w_a2, b_a2=b_gla_a[l].reshape(1, qk), norm_g=gla_norm_g[l].reshape(1, gv),
            w_ba=w_branch_a[l].astype(BF16), w_bb=w_branch_b[l].astype(BF16), w_out=w_out[l].astype(BF16),
            ln1_g=ln1_g[l], ln1_b=ln1_b[l], w_rt=w_rt, b_rt=b_rt,
            w_eg=w_exp_gate[l].astype(BF16), w_eu=w_exp_up[l].astype(BF16), w_ed=w_exp_down[l].astype(BF16),
            ln2_g=ln2_g[l], ln2_b=ln2_b[l])
        mod = _cond(jnp.concatenate([c_prompt, c_sample], axis=0), w_cond[l].astype(BF16), b_cond[l])
        s0p = jnp.zeros((bp, nh, dk, dv), F32)
        yp, kp, vp, sp = _layer(yp, mod[:bp], wts, None, s0p)
        past = (cache_sb_k[l].reshape(bs, -1, sbw), cache_sb_v[l].reshape(bs, -1, sbw))
        ys, ks, vs, ss = _layer(ys, mod[bp:], wts, past, state_gla[l])
        for lst, val in zip(outs, (kp.reshape(bp, tp, sbh, sbd), vp.reshape(bp, tp, sbh, sbd), sp,
                                   ks.reshape(bs, ts, sbh, sbd), vs.reshape(bs, ts, sbh, sbd), ss)):
            lst.append(val)
    return (yp, ys) + tuple(jnp.stack(v) for v in outs)
```

```python
import functools
import math

import jax
import jax.numpy as jnp
from jax import lax
from jax.experimental import pallas as pl
from jax.experimental.pallas import tpu as pltpu

F32 = jnp.float32
BF16 = jnp.bfloat16
I32 = jnp.int32

LN_EPS = 1e-5
GLA_HEADS = 4
GLA_CHUNK = 64
GLA_TAU = 16.0
GLA_LOW_RANK = 16
SB_HEADS = 16
N_GROUPS = 4
EXPERTS_PER_GROUP = 8
N_EXPERTS = N_GROUPS * EXPERTS_PER_GROUP
LANES = 128
VMEM_LIMIT = 56 * 1024 * 1024
FFN_TILE = 256
NEG_BIG = -1e30


def _cparams(sem, vmem=VMEM_LIMIT):
    return pltpu.CompilerParams(dimension_semantics=sem, vmem_limit_bytes=vmem)


def _iota(shape, dim):
    return lax.broadcasted_iota(I32, shape, dim)


def _sigmoid(x):
    return 1.0 / (1.0 + jnp.exp(-x))


def _silu(x):
    return x * _sigmoid(x)


def _softplus(x):
    return jnp.maximum(x, 0.0) + jnp.log1p(jnp.exp(-jnp.abs(x)))


def _ln(x):
    mu = jnp.mean(x, axis=-1, keepdims=True)
    xc = x - mu
    var = jnp.mean(xc * xc, axis=-1, keepdims=True)
    return xc * lax.rsqrt(var + LN_EPS)


def _split_bf16(x):
    hi = x.astype(BF16)
    lo = (x - hi.astype(F32)).astype(BF16)
    return hi, lo


def _dot(a, b):
    return jnp.dot(a, b, preferred_element_type=F32)


def _dot_nt(a, b):
    return lax.dot_general(a, b, (((1,), (1,)), ((), ())), preferred_element_type=F32)


def _dot_tn(a, b):
    return lax.dot_general(a, b, (((0,), (0,)), ((), ())), preferred_element_type=F32)


def _cond_body(c_ref, w_ref, b_ref, o_ref):
    c = _silu(c_ref[...]).astype(BF16)
    o_ref[...] = _dot(c, w_ref[...]) + b_ref[...]


def _cond(c, w_bf, b):
    rows, d = c.shape
    n = w_bf.shape[1]
    tn = 1024
    return pl.pallas_call(
        _cond_body,
        out_shape=jax.ShapeDtypeStruct((rows, n), F32),
        grid=(n // tn,),
        in_specs=[pl.BlockSpec((rows, d), lambda j: (0, 0)),
                  pl.BlockSpec((d, tn), lambda j: (0, j)),
                  pl.BlockSpec((1, tn), lambda j: (0, j))],
        out_specs=pl.BlockSpec((rows, tn), lambda j: (0, j)),
        compiler_params=_cparams(("arbitrary",)),
        name="cond",
    )(c, w_bf, b.reshape(1, n))


def _lnmod_body(x_ref, sc_ref, sh_ref, o_ref):
    x = x_ref[...]
    h = _ln(x) * (1.0 + sc_ref[...]) + sh_ref[...]
    o_ref[...] = h.astype(o_ref.dtype)


def _row_tiles(bsz, t, target):
    if t >= target:
        return 1, target
    return min(bsz, target // t), t


def _lnmod(x, scale, shift):
    bsz, t, d = x.shape
    tb, tt = _row_tiles(bsz, t, 512)
    return pl.pallas_call(
        _lnmod_body,
        out_shape=jax.ShapeDtypeStruct((bsz, t, d), BF16),
        grid=(bsz // tb, t // tt),
        in_specs=[pl.BlockSpec((tb, tt, d), lambda i, j: (i, j, 0)),
                  pl.BlockSpec((tb, 1, d), lambda i, j: (i, 0, 0)),
                  pl.BlockSpec((tb, 1, d), lambda i, j: (i, 0, 0))],
        out_specs=pl.BlockSpec((tb, tt, d), lambda i, j: (i, j, 0)),
        compiler_params=_cparams(("arbitrary", "arbitrary")),
        name="lnmod",
    )(x, scale.reshape(bsz, 1, d), shift.reshape(bsz, 1, d))


def _mm_body(x_ref, w_ref, o_ref):
    o_ref[...] = _dot(x_ref[...], w_ref[...]).astype(o_ref.dtype)


def _matmul(x, w, out_dtype, name):
    n, k = x.shape
    m = w.shape[1]
    tm = min(n, 1024)
    tn = 2048 if m % 2048 == 0 else (1024 if m % 1024 == 0 else m)
    return pl.pallas_call(
        _mm_body,
        out_shape=jax.ShapeDtypeStruct((n, m), out_dtype),
        grid=(n // tm, m // tn),
        in_specs=[pl.BlockSpec((tm, k), lambda i, j: (i, 0)),
                  pl.BlockSpec((k, tn), lambda i, j: (0, j))],
        out_specs=pl.BlockSpec((tm, tn), lambda i, j: (i, j)),
        compiler_params=_cparams(("arbitrary", "arbitrary")),
        name=name,
    )(x, w)


def _gla_body(q_ref, k_ref, v_ref, g_ref, lr_ref, w2_ref, b2_ref, ng_ref, s0_ref,
              o_ref, sout_ref, st_ref, *, n_chunks, dk, dv):
    j = pl.program_id(1)
    nh = GLA_HEADS
    c = GLA_CHUNK

    @pl.when(j == 0)
    def _():
        st_ref[...] = s0_ref[0]

    row = _iota((c, c), 0)
    col = _iota((c, c), 1)
    tril = (row >= col)
    tril_bf = jnp.where(tril, 1.0, 0.0).astype(BF16)
    eye_k = _iota((dk, dk), 0) == _iota((dk, dk), 1)

    def chunk(ci, carry):
        r0 = pl.multiple_of(ci * c, c)
        sl = pl.ds(r0, c)
        lr = lr_ref[0, sl, :].astype(BF16)
        pre = _dot(lr, w2_ref[...]) + b2_ref[...]
        la = -_softplus(-pre) * (1.0 / GLA_TAU)
        hi, lo = _split_bf16(la)
        bc = _dot(tril_bf, hi) + _dot(tril_bf, lo)
        b_last = bc[c - 1:c, :]
        q = q_ref[0, sl, :].astype(F32) * (dk ** -0.5)
        k = k_ref[0, sl, :].astype(F32)
        qe = (q * jnp.exp(bc)).astype(BF16)
        ke = (k * jnp.exp(-bc)).astype(BF16)
        kd = (k * jnp.exp(b_last - bc)).astype(BF16)
        e_last = jnp.exp(b_last)
        for h in range(nh):
            ks = slice(h * dk, (h + 1) * dk)
            vs = slice(h * dv, (h + 1) * dv)
            v_h = v_ref[0, sl, vs]
            s_h = st_ref[h]
            o_inter = _dot(qe[:, ks], s_h.astype(BF16))
            sc = _dot_nt(qe[:, ks], ke[:, ks])
            sc = jnp.where(tril, sc, 0.0).astype(BF16)
            o = o_inter + _dot(sc, v_h)
            e_col = jnp.sum(jnp.where(eye_k, jnp.broadcast_to(e_last[:, ks], (dk, dk)), 0.0),
                            axis=1, keepdims=True)
            st_ref[h] = e_col * s_h + _dot_tn(kd[:, ks], v_h)
            ms = jnp.mean(o * o, axis=-1, keepdims=True)
            on = o * lax.rsqrt(ms + LN_EPS) * ng_ref[:, vs]
            gate = g_ref[0, sl, vs].astype(F32)
            o_ref[0, sl, vs] = (on * _silu(gate)).astype(o_ref.dtype)
        return carry

    lax.fori_loop(0, n_chunks, chunk, 0)

    @pl.when(j == pl.num_programs(1) - 1)
    def _():
        sout_ref[0] = st_ref[...]


def _gla(proj, lra, w2p, b2, ng, s0, *, q_col, k_col, v_col, g_col):
    bsz, t, _ = proj.shape
    nh, dk, dv = s0.shape[1], s0.shape[2], s0.shape[3]
    tt = min(t, 256)
    n_chunks = tt // GLA_CHUNK
    body = functools.partial(_gla_body, n_chunks=n_chunks, dk=dk, dv=dv)
    return pl.pallas_call(
        body,
        out_shape=(jax.ShapeDtypeStruct((bsz, t, nh * dv), BF16),
                   jax.ShapeDtypeStruct((bsz, nh, dk, dv), F32)),
        grid=(bsz, t // tt),
        in_specs=[pl.BlockSpec((1, tt, nh * dk), lambda b, j: (b, j, q_col)),
                  pl.BlockSpec((1, tt, nh * dk), lambda b, j: (b, j, k_col)),
                  pl.BlockSpec((1, tt, nh * dv), lambda b, j: (b, j, v_col)),
                  pl.BlockSpec((1, tt, nh * dv), lambda b, j: (b, j, g_col)),
                  pl.BlockSpec((1, tt, LANES), lambda b, j: (b, j, 0)),
                  pl.BlockSpec((LANES, nh * dk), lambda b, j: (0, 0)),
                  pl.BlockSpec((1, nh * dk), lambda b, j: (0, 0)),
                  pl.BlockSpec((1, nh * dv), lambda b, j: (0, 0)),
                  pl.BlockSpec((1, nh, dk, dv), lambda b, j: (b, 0, 0, 0))],
        out_specs=(pl.BlockSpec((1, tt, nh * dv), lambda b, j: (b, j, 0)),
                   pl.BlockSpec((1, nh, dk, dv), lambda b, j: (b, 0, 0, 0))),
        scratch_shapes=[pltpu.VMEM((nh, dk, dv), F32)],
        compiler_params=_cparams(("arbitrary", "arbitrary")),
        name="gla",
    )(proj, proj, proj, proj, lra, w2p, b2, ng, s0)


def _sb_body(*refs, tq, tk, t_new, t_past, hd):
    if t_past:
        (q_ref, k_ref, v_ref, kp_ref, vp_ref, o_ref,
         nz_ref, wg_ref, r_ref, acc_ref, k2n_ref, v2n_ref, k2p_ref, v2p_ref) = refs
    else:
        q_ref, k_ref, v_ref, o_ref, nz_ref, wg_ref, r_ref, acc_ref, k2n_ref, v2n_ref = refs
    qi = pl.program_id(2)
    wd = min(tk, tq)
    n_new = t_new // wd
    n_past = t_past // tk
    head0 = _iota((1, LANES), 1) < hd

    def expand(x):
        x = x.astype(BF16)
        zero = jnp.zeros_like(x)
        return jnp.concatenate([jnp.where(head0, x, zero), jnp.where(head0, zero, x)], axis=0)

    @pl.when(qi == 0)
    def _():
        def fill_new(j, c):
            rows = pl.ds(pl.multiple_of(j * wd, wd), wd)
            k2n_ref[j] = expand(k_ref[0, rows, :])
            v2n_ref[j] = expand(v_ref[0, rows, :])
            return c
        lax.fori_loop(0, n_new, fill_new, 0)
        if t_past:
            def fill_past(j, c):
                rows = pl.ds(pl.multiple_of(j * tk, tk), tk)
                k2p_ref[j] = expand(kp_ref[0, rows, :])
                v2p_ref[j] = expand(vp_ref[0, rows, :])
                return c
            lax.fori_loop(0, n_past, fill_past, 0)

    qn = (q_ref[0].astype(F32) * (-(hd ** -0.5))).astype(BF16)

    def suffix_matrix(w):
        r = _iota((2 * w, 2 * w), 0)
        c = _iota((2 * w, 2 * w), 1)
        same = (r < w) == (c < w)
        return jnp.where(same & (r >= c), 1.0, 0.0).astype(BF16)

    uu = {w: suffix_matrix(w) for w in {wd, tk}}
    first = {w: _iota((1, 2 * w), 1) < w for w in {wd, tk}}

    def head_sums(x, w):
        if w % LANES == 0:
            return (jnp.sum(x[:, :w], axis=1, keepdims=True), jnp.sum(x[:, w:], axis=1, keepdims=True))
        return (jnp.sum(jnp.where(first[w], x, 0.0), axis=1, keepdims=True),
                jnp.sum(jnp.where(first[w], 0.0, x), axis=1, keepdims=True))

    def both_heads(r0, r1, w):
        if w % LANES == 0:
            reps = w // LANES
            return jnp.concatenate([r0] * reps + [r1] * reps, axis=1)
        return jnp.where(first[w], r0[:, :2 * w], r1[:, :2 * w])

    def scores(k2cat):
        return _dot_nt(qn, k2cat)

    def weights(slot, width, w, masks):
        n = width // (2 * w)
        staged = []
        for b in range(n - 1, -1, -1):
            x = nz_ref[slot, :, b * 2 * w:(b + 1) * 2 * w]
            l1m = jnp.minimum(x, 0.0) - jnp.log(1.0 + jnp.exp(-jnp.abs(x)))
            if masks is not None:
                l1m = jnp.where(masks[b], l1m, 0.0)
            incl = _dot(l1m.astype(BF16), uu[w])
            staged.append((b, incl - x, None, head_sums(l1m, w)))
        r0 = r_ref[0]
        r1 = r_ref[1]
        for b, logw, _, (s0, s1) in staged:
            wgt = jnp.exp(logw + both_heads(r0, r1, w))
            if masks is not None:
                wgt = jnp.where(masks[b], wgt, 0.0)
            wg_ref[slot, :, b * 2 * w:(b + 1) * 2 * w] = wgt.astype(BF16)
            r0 = r0 + s0
            r1 = r1 + s1
        r_ref[0] = r0
        r_ref[1] = r1

    n_diag = tq // wd
    gw = n_diag * 2 * wd

    def new_group(ref, top):
        return ref[pl.ds(top - (n_diag - 1), n_diag)].reshape(gw, LANES)

    qpos = _iota((tq, 2 * wd), 0)
    kcol = _iota((tq, 2 * wd), 1)
    kcol = jnp.where(kcol < wd, kcol, kcol - wd)
    masks = [(kcol + dblk * wd) < qpos for dblk in range(n_diag)]
    top0 = (qi + 1) * n_diag - 1
    r_ref[...] = jnp.zeros_like(r_ref)
    nz_ref[0, :, :gw] = scores(new_group(k2n_ref, top0))

    if t_new > tq:
        def top_of(g):
            return top0 - g * n_diag

        nz_ref[1, :, :gw] = scores(new_group(k2n_ref, jnp.maximum(top_of(1), n_diag - 1)))
        weights(0, gw, wd, masks)
        acc_ref[...] = jnp.zeros_like(acc_ref)

        def older(g, c):
            for cur in (0, 1):
                @pl.when(g % 2 == cur)
                def _():
                    nz_ref[1 - cur, :, :gw] = scores(
                        new_group(k2n_ref, jnp.maximum(top_of(g + 1), n_diag - 1)))
                    acc_ref[...] += _dot(wg_ref[1 - cur, :, :gw], new_group(v2n_ref, top_of(g - 1)))
                    weights(cur, gw, wd, None)
            return c

        lax.fori_loop(1, qi + 1, older, 0)
        acc_ref[...] += _dot(wg_ref[qi % 2, :, :gw], new_group(v2n_ref, top_of(qi)))
    else:
        weights(0, gw, wd, masks)
        acc_ref[...] = _dot(wg_ref[0, :, :gw], new_group(v2n_ref, top0))

    if t_past:
        gp = min(n_past, 8)
        pw = gp * 2 * tk

        def past(i, c):
            lo_blk = n_past - (i + 1) * gp
            nz_ref[0, :, :pw] = scores(k2p_ref[pl.ds(lo_blk, gp)].reshape(pw, LANES))
            weights(0, pw, tk, None)
            acc_ref[...] += _dot(wg_ref[0, :, :pw], v2p_ref[pl.ds(lo_blk, gp)].reshape(pw, LANES))
            return c
        lax.fori_loop(0, n_past // gp, past, 0)

    o_ref[0] = acc_ref[...].astype(o_ref.dtype)


def _sb(proj, q_col, k_new, v_new, k_past=None, v_past=None):
    bsz, t, _ = proj.shape
    width = k_new.shape[-1]
    hd = width // SB_HEADS
    n_pairs = width // LANES
    tq = min(t, 256)
    tk = 128
    t_past = 0 if k_past is None else k_past.shape[1]
    body = functools.partial(_sb_body, tq=tq, tk=tk, t_new=t, t_past=t_past, hd=hd)
    in_specs = [pl.BlockSpec((1, tq, LANES), lambda b, p, i: (b, i, q_col + p)),
                pl.BlockSpec((1, t, LANES), lambda b, p, i: (b, 0, p)),
                pl.BlockSpec((1, t, LANES), lambda b, p, i: (b, 0, p))]
    args = [proj, k_new, v_new]
    wd = min(tk, tq)
    group_w = (tq // wd) * 2 * wd
    if t_past:
        group_w = max(group_w, min(t_past // tk, 8) * 2 * tk)
    scratch = [pltpu.VMEM((2, tq, group_w), F32), pltpu.VMEM((2, tq, group_w), BF16),
               pltpu.VMEM((2, tq, LANES), F32), pltpu.VMEM((tq, LANES), F32),
               pltpu.VMEM((t // wd, 2 * wd, LANES), BF16), pltpu.VMEM((t // wd, 2 * wd, LANES), BF16)]
    if t_past:
        in_specs += [pl.BlockSpec((1, t_past, LANES), lambda b, p, i: (b, 0, p)),
                     pl.BlockSpec((1, t_past, LANES), lambda b, p, i: (b, 0, p))]
        args += [k_past, v_past]
        scratch += [pltpu.VMEM((t_past // tk, 2 * tk, LANES), BF16),
                    pltpu.VMEM((t_past // tk, 2 * tk, LANES), BF16)]
    return pl.pallas_call(
        body,
        out_shape=jax.ShapeDtypeStruct((bsz, t, width), BF16),
        grid=(bsz, n_pairs, t // tq),
        in_specs=in_specs,
        out_specs=pl.BlockSpec((1, tq, LANES), lambda b, p, i: (b, i, p)),
        scratch_shapes=scratch,
        compiler_params=_cparams(("arbitrary", "arbitrary", "arbitrary")),
        name="stickbreak",
    )(*args)


def _merge_body(oa_ref, ob_ref, ga_ref, gb_ref, x_ref, wa_ref, wb_ref, wo_ref,
                gm_ref, sf_ref, cf_ref, l1g_ref, l1b_ref, x1_ref, h2_ref, *, alpha):
    tb, tt, d = x_ref.shape
    n = tb * tt
    a = _dot(oa_ref[...].reshape(n, d), wa_ref[...])
    b = _dot(ob_ref[...].reshape(n, d), wb_ref[...])
    ga = _sigmoid(ga_ref[...].reshape(n, d).astype(F32))
    gb = _sigmoid(gb_ref[...].reshape(n, d).astype(F32))
    merged = (ga * a + gb * b).astype(BF16)
    mix = _dot(merged, wo_ref[...]).reshape(tb, tt, d)
    x = x_ref[...]
    x1 = _ln(alpha * x + gm_ref[...] * mix) * l1g_ref[...] + l1b_ref[...]
    x1_ref[...] = x1
    h2 = _ln(x1) * (1.0 + cf_ref[...]) + sf_ref[...]
    h2_ref[...] = h2.astype(BF16).astype(F32)


def _merge(oa, ob, proj, ga_col, gb_col, x, wa, wb, wo, gate_m, shift_f, scale_f, l1g, l1b, alpha):
    bsz, t, d = x.shape
    tb, tt = _row_tiles(bsz, t, 512)
    tok = lambda i, j: (i, j, 0)
    per_b = lambda i, j: (i, 0, 0)
    const2 = lambda i, j: (0, 0)
    const3 = lambda i, j: (0, 0, 0)
    body = functools.partial(_merge_body, alpha=alpha)
    return pl.pallas_call(
        body,
        out_shape=(jax.ShapeDtypeStruct((bsz, t, d), F32), jax.ShapeDtypeStruct((bsz, t, d), F32)),
        grid=(bsz // tb, t // tt),
        in_specs=[pl.BlockSpec((tb, tt, d), tok),
                  pl.BlockSpec((tb, tt, d), tok),
                  pl.BlockSpec((tb, tt, d), lambda i, j: (i, j, ga_col)),
                  pl.BlockSpec((tb, tt, d), lambda i, j: (i, j, gb_col)),
                  pl.BlockSpec((tb, tt, d), tok),
                  pl.BlockSpec((d, d), const2), pl.BlockSpec((d, d), const2), pl.BlockSpec((d, d), const2),
                  pl.BlockSpec((tb, 1, d), per_b), pl.BlockSpec((tb, 1, d), per_b),
                  pl.BlockSpec((tb, 1, d), per_b),
                  pl.BlockSpec((1, 1, d), const3), pl.BlockSpec((1, 1, d), const3)],
        out_specs=(pl.BlockSpec((tb, tt, d), tok), pl.BlockSpec((tb, tt, d), tok)),
        compiler_params=_cparams(("arbitrary", "arbitrary")),
        name="merge",
    )(oa, ob, proj, proj, x, wa, wb, wo,
      gate_m.reshape(bsz, 1, d), shift_f.reshape(bsz, 1, d), scale_f.reshape(bsz, 1, d),
      l1g.reshape(1, 1, d), l1b.reshape(1, 1, d))


def _router_body(h_ref, w_ref, b_ref, route_ref, cnt_ref, run_ref, *, tm):
    i = pl.program_id(0)

    @pl.when(i == 0)
    def _():
        run_ref[...] = jnp.zeros_like(run_ref)

    logits = _dot(h_ref[...].astype(BF16), w_ref[...]) + b_ref[...]
    lane = _iota((tm, LANES), 1).astype(F32)
    is_g = lane < N_GROUPS
    is_e = (lane >= N_GROUPS) & (lane < N_GROUPS + N_EXPERTS)

    def amax(mask):
        m = jnp.max(jnp.where(mask, logits, NEG_BIG), axis=1, keepdims=True)
        idx = jnp.min(jnp.where(mask & (logits == m), lane, float(LANES)), axis=1, keepdims=True)
        return m, idx

    gm, gi = amax(is_g)
    p_top = 1.0 / jnp.sum(jnp.where(is_g, jnp.exp(logits - gm), 0.0), axis=1, keepdims=True)
    lo = N_GROUPS + gi * EXPERTS_PER_GROUP
    in_grp = is_e & (lane >= lo) & (lane < lo + EXPERTS_PER_GROUP)
    v1, i1 = amax(in_grp)
    v2, i2 = amax(in_grp & (lane != i1))
    e21 = jnp.exp(v2 - v1)
    w1 = p_top / (1.0 + e21)
    w2 = p_top * e21 / (1.0 + e21)
    e1 = i1 - N_GROUPS
    e2 = i2 - N_GROUPS

    oh = jnp.where((lane == e1) | (lane == e2), 1.0, 0.0).astype(BF16)
    tri = jnp.where(_iota((tm, tm), 0) >= _iota((tm, tm), 1), 1.0, 0.0).astype(BF16)
    cnt = _dot(tri, oh) + run_ref[...]
    rank1 = jnp.sum(jnp.where(lane == e1, cnt, 0.0), axis=1, keepdims=True) - 1.0
    rank2 = jnp.sum(jnp.where(lane == e2, cnt, 0.0), axis=1, keepdims=True) - 1.0
    run_ref[...] = cnt[tm - 1:tm, :]
    cnt_ref[...] = cnt[tm - 1:tm, :]

    out = jnp.where(lane == 0, e1, 0.0)
    out = jnp.where(lane == 1, e2, out)
    out = jnp.where(lane == 2, w1, out)
    out = jnp.where(lane == 3, w2, out)
    out = jnp.where(lane == 4, rank1, out)
    out = jnp.where(lane == 5, rank2, out)
    route_ref[...] = out


def _router(h2, w_rt, b_rt):
    n, d = h2.shape
    tm = min(n, 512)
    body = functools.partial(_router_body, tm=tm)
    return pl.pallas_call(
        body,
        out_shape=(jax.ShapeDtypeStruct((n, LANES), F32), jax.ShapeDtypeStruct((1, LANES), F32)),
        grid=(n // tm,),
        in_specs=[pl.BlockSpec((tm, d), lambda i: (i, 0)),
                  pl.BlockSpec((d, LANES), lambda i: (0, 0)),
                  pl.BlockSpec((1, LANES), lambda i: (0, 0))],
        out_specs=(pl.BlockSpec((tm, LANES), lambda i: (i, 0)), pl.BlockSpec((1, LANES), lambda i: (0, 0))),
        scratch_shapes=[pltpu.VMEM((1, LANES), F32)],
        compiler_params=_cparams(("arbitrary",)),
        name="router",
    )(h2, w_rt, b_rt)


def _dispatch_body(pos_ref, h_ref, hs_ref, sem, *, tm):
    i = pl.program_id(0)
    base = i * tm

    def copy(t, slot):
        dst = pos_ref[2 * (base + t) + slot]
        return pltpu.make_async_copy(h_ref.at[pl.ds(t, 1)], hs_ref.at[pl.ds(dst, 1)], sem)

    def start(t, c):
        copy(t, 0).start()
        copy(t, 1).start()
        return c

    lax.fori_loop(0, tm, start, 0, unroll=8)
    for _ in range(2):
        pltpu.make_async_copy(h_ref, hs_ref.at[pl.ds(0, tm)], sem).wait()


def _dispatch(pos, h2, n_rows):
    n, d = h2.shape
    tm = min(n, 256)
    body = functools.partial(_dispatch_body, tm=tm)
    grid_spec = pltpu.PrefetchScalarGridSpec(
        num_scalar_prefetch=1,
        grid=(n // tm,),
        in_specs=[pl.BlockSpec((tm, d), lambda i, pos: (i, 0))],
        out_specs=pl.BlockSpec(memory_space=pl.ANY),
        scratch_shapes=[pltpu.SemaphoreType.DMA],
    )
    return pl.pallas_call(
        body,
        out_shape=jax.ShapeDtypeStruct((n_rows, d), h2.dtype),
        grid_spec=grid_spec,
        compiler_params=pltpu.CompilerParams(dimension_semantics=("arbitrary",),
                                             vmem_limit_bytes=VMEM_LIMIT, has_side_effects=True),
        name="dispatch",
    )(pos, h2)


def _ffn_body(meta_ref, x_ref, wg_ref, wu_ref, wd_ref, o_ref, *, tm, n_vis):
    v = pl.program_id(0)
    nv = meta_ref[0]

    @pl.when(v < nv)
    def _():
        tile = meta_ref[1 + v]
        e = meta_ref[1 + n_vis + v]
        first = meta_ref[1 + 2 * n_vis + v]
        lo = meta_ref[1 + 3 * n_vis + e]
        hi = meta_ref[2 + 3 * n_vis + e]
        x = x_ref[...].astype(BF16)
        g = _dot(x, wg_ref[0])
        u = _dot(x, wu_ref[0])
        he = (_silu(g) * u).astype(BF16)
        y = _dot(he, wd_ref[0])
        row = tile * tm + _iota((tm, 1), 0)
        mine = (row >= lo) & (row < hi)

        @pl.when(first == 1)
        def _():
            o_ref[...] = jnp.where(mine, y, 0.0)

        @pl.when(first == 0)
        def _():
            o_ref[...] = jnp.where(mine, y, o_ref[...])


def _ffn(meta, hs, wg, wu, wd, n_vis):
    n_rows, d = hs.shape
    ff = wg.shape[-1]
    tm = FFN_TILE
    body = functools.partial(_ffn_body, tm=tm, n_vis=n_vis)
    tile_of = lambda v, m: (m[1 + jnp.minimum(v, m[0] - 1)], 0)
    exp_of = lambda v, m: (m[1 + n_vis + jnp.minimum(v, m[0] - 1)], 0, 0)
    grid_spec = pltpu.PrefetchScalarGridSpec(
        num_scalar_prefetch=1,
        grid=(n_vis,),
        in_specs=[pl.BlockSpec((tm, d), tile_of),
                  pl.BlockSpec((1, d, ff), exp_of),
                  pl.BlockSpec((1, d, ff), exp_of),
                  pl.BlockSpec((1, ff, d), exp_of)],
        out_specs=pl.BlockSpec((tm, d), tile_of),
    )
    return pl.pallas_call(
        body,
        out_shape=jax.ShapeDtypeStruct((n_rows, d), F32),
        grid_spec=grid_spec,
        compiler_params=_cparams(("arbitrary",)),
        name="expert_ffn",
    )(meta, hs, wg, wu, wd)


def _combine_body(pos_ref, ys_ref, route_ref, x1_ref, gf_ref, g_ref, b_ref, o_ref, buf, sem,
                  *, tm, alpha):
    i = pl.program_id(0)
    base = i * tm

    def copy(t, slot):
        src = pos_ref[2 * (base + t) + slot]
        return pltpu.make_async_copy(ys_ref.at[pl.ds(src, 1)], buf.at[slot, pl.ds(t, 1)], sem)

    def start(t, c):
        copy(t, 0).start()
        copy(t, 1).start()
        return c

    lax.fori_loop(0, tm, start, 0, unroll=8)
    for slot in range(2):
        pltpu.make_async_copy(ys_ref.at[pl.ds(0, tm)], buf.at[slot], sem).wait()
    tb, tt, d = x1_ref.shape
    route = route_ref[...]
    w1 = route[:, 2:3]
    w2 = route[:, 3:4]
    moe = (w1 * buf[0] + w2 * buf[1]).reshape(tb, tt, d)
    y = _ln(alpha * x1_ref[...] + gf_ref[...] * moe) * g_ref[...] + b_ref[...]
    o_ref[...] = y


def _combine(pos, ys, route, x1, gate_f, l2g, l2b, alpha):
    bsz, t, d = x1.shape
    tb, tt = _row_tiles(bsz, t, 256)
    tm = tb * tt
    nj = t // tt
    body = functools.partial(_combine_body, tm=tm, alpha=alpha)
    grid_spec = pltpu.PrefetchScalarGridSpec(
        num_scalar_prefetch=1,
        grid=(bsz * t // tm,),
        in_specs=[pl.BlockSpec(memory_space=pl.ANY),
                  pl.BlockSpec((tm, LANES), lambda i, pos: (i, 0)),
                  pl.BlockSpec((tb, tt, d), lambda i, pos: (i // nj, i % nj, 0)),
                  pl.BlockSpec((tb, 1, d), lambda i, pos: (i // nj, 0, 0)),
                  pl.BlockSpec((1, 1, d), lambda i, pos: (0, 0, 0)),
                  pl.BlockSpec((1, 1, d), lambda i, pos: (0, 0, 0))],
        out_specs=pl.BlockSpec((tb, tt, d), lambda i, pos: (i // nj, i % nj, 0)),
        scratch_shapes=[pltpu.VMEM((2, tm, d), F32), pltpu.SemaphoreType.DMA],
    )
    return pl.pallas_call(
        body,
        out_shape=jax.ShapeDtypeStruct((bsz, t, d), F32),
        grid_spec=grid_spec,
        compiler_params=_cparams(("arbitrary",)),
        name="combine",
    )(pos, ys, route, x1, gate_f.reshape(bsz, 1, d), l2g.reshape(1, 1, d), l2b.reshape(1, 1, d))


def _routing_tables(route, counts, n_tokens, n_vis):
    tm = FFN_TILE
    cnt = counts[0, :N_EXPERTS].astype(I32)
    ends = jnp.cumsum(cnt)
    offs = jnp.concatenate([jnp.zeros((1,), I32), ends])
    eid = route[:, 0:2].astype(I32)
    rank = route[:, 4:6].astype(I32)
    onehot = eid[:, :, None] == jnp.arange(N_EXPERTS, dtype=I32)[None, None, :]
    pos = rank + jnp.sum(jnp.where(onehot, offs[None, None, :N_EXPERTS], 0), axis=-1)
    pos = pos.reshape(-1)

    n_tiles = (2 * n_tokens) // tm
    t_lo = jnp.arange(n_tiles, dtype=I32)[:, None] * tm
    inter = (offs[None, :-1] < t_lo + tm) & (offs[None, 1:] > t_lo) & (cnt[None, :] > 0)
    flat = inter.reshape(-1)
    order = jnp.argsort(jnp.where(flat, 0, 1), stable=True)[:n_vis].astype(I32)
    nv = jnp.sum(flat.astype(I32))
    v_tile = order // N_EXPERTS
    v_exp = order % N_EXPERTS
    valid = jnp.arange(n_vis, dtype=I32) < nv
    v_tile = jnp.where(valid, v_tile, 0)
    v_exp = jnp.where(valid, v_exp, 0)
    prev_tile = jnp.concatenate([jnp.full((1,), -1, I32), v_tile[:-1]])
    v_first = (v_tile != prev_tile).astype(I32)
    meta = jnp.concatenate([nv[None], v_tile, v_exp, v_first, offs]).astype(I32)
    return pos, meta


def _layer(x, mod, wts, sb_past, s0):
    bsz, t, d = x.shape
    n = bsz * t
    shift_m, scale_m, gate_m, shift_f, scale_f, gate_f = [mod[:, i * d:(i + 1) * d] for i in range(6)]
    alpha = wts["alpha"]

    h = _lnmod(x, scale_m, shift_m).reshape(n, d)
    proj = _matmul(h, wts["w_main"], BF16, "proj_main").reshape(bsz, t, -1)
    kb = _matmul(h, wts["w_kb"], F32, "proj_kb")
    vb = _matmul(h, wts["w_vb"], F32, "proj_vb")
    lra = _matmul(h, wts["w_lr"], F32, "proj_lr").reshape(bsz, t, LANES)

    oa, s_new = _gla(proj, lra, wts["w_a2"], wts["b_a2"], wts["norm_g"], s0,
                     q_col=0, k_col=1, v_col=1, g_col=2)
    kb3 = kb.reshape(bsz, t, d)
    vb3 = vb.reshape(bsz, t, d)
    if sb_past is None:
        ob = _sb(proj, 3072 // LANES, kb3, vb3)
    else:
        ob = _sb(proj, 3072 // LANES, kb3, vb3, sb_past[0], sb_past[1])

    x1, h2 = _merge(oa, ob, proj, 4, 5, x, wts["w_ba"], wts["w_bb"], wts["w_out"],
                    gate_m, shift_f, scale_f, wts["ln1_g"], wts["ln1_b"], alpha)
    h2 = h2.reshape(n, d)

    route, counts = _router(h2, wts["w_rt"], wts["b_rt"])
    n_vis = (2 * n) // FFN_TILE + N_EXPERTS - 1
    pos, meta = _routing_tables(route, counts, n, n_vis)
    hs = _dispatch(pos, h2, 2 * n)
    ys = _ffn(meta, hs, wts["w_eg"], wts["w_eu"], wts["w_ed"], n_vis)
    y = _combine(pos, ys, route, x1, gate_f, wts["ln2_g"], wts["ln2_b"], alpha)
    return y, kb, vb, s_new


def kernel(x_prompt, x_sample, cache_sb_k, cache_sb_v, state_gla, c_prompt, c_sample, w_cond, b_cond, w_in, w_gla_a2, b_gla_a, gla_norm_g, w_branch_a, w_branch_b, w_out, ln1_g, ln1_b, w_group, b_group, w_router, b_router, w_exp_gate, w_exp_up, w_exp_down, ln2_g, ln2_b):
    depth = w_in.shape[0]
    d = x_prompt.shape[-1]
    bp, tp, _ = x_prompt.shape
    bs, ts, _ = x_sample.shape
    nh, dk, dv = state_gla.shape[2], state_gla.shape[3], state_gla.shape[4]
    qk = nh * dk
    gv = nh * dv
    sbh, sbd = cache_sb_k.shape[3], cache_sb_k.shape[4]
    sbw = sbh * sbd
    alpha = (2.0 * depth) ** 0.25

    yp, ys = x_prompt, x_sample
    outs = [[] for _ in range(6)]
    for l in range(depth):
        wi = w_in[l]
        o = [0]
        for wdt in (qk, qk, gv, gv, GLA_LOW_RANK, sbw, sbw, sbw, d, d):
            o.append(o[-1] + wdt)
        col = lambda i: wi[:, o[i]:o[i + 1]]
        w_main = jnp.concatenate([col(0), col(1), col(2), col(3), col(5), col(8), col(9)], axis=1).astype(BF16)
        w_lr = jnp.pad(col(4), ((0, 0), (0, LANES - GLA_LOW_RANK))).astype(BF16)
        w_a2 = jnp.pad(w_gla_a2[l], ((0, LANES - GLA_LOW_RANK), (0, 0))).astype(BF16)
        w_rt = jnp.pad(jnp.concatenate([w_group[l], w_router[l]], axis=1),
                       ((0, 0), (0, LANES - N_GROUPS - N_EXPERTS))).astype(BF16)
        b_rt = jnp.pad(jnp.concatenate([b_group[l], b_router[l]]), (0, LANES - N_GROUPS - N_EXPERTS)).reshape(1, LANES)
        wts = dict(
            alpha=alpha, w_main=w_main, w_kb=col(6).astype(BF16), w_vb=col(7).astype(BF16), w_lr=w_lr,
            w_a2=w_a2, b_a2=b_gla_a[l].reshape(1, qk), norm_g=gla_norm_g[l].reshape(1, gv),
            w_ba=w_branch_a[l].astype(BF16), w_bb=w_branch_b[l].astype(BF16), w_out=w_out[l].astype(BF16),
            ln1_g=ln1_g[l], ln1_b=ln1_b[l], w_rt=w_rt, b_rt=b_rt,
            w_eg=w_exp_gate[l].astype(BF16), w_eu=w_exp_up[l].astype(BF16), w_ed=w_exp_down[l].astype(BF16),
            ln2_g=ln2_g[l], ln2_b=ln2_b[l])
        mod = _cond(jnp.concatenate([c_prompt, c_sample], axis=0), w_cond[l].astype(BF16), b_cond[l])
        s0p = jnp.zeros((bp, nh, dk, dv), F32)
        yp, kp, vp, sp = _layer(yp, mod[:bp], wts, None, s0p)
        past = (cache_sb_k[l].reshape(bs, -1, sbw), cache_sb_v[l].reshape(bs, -1, sbw))
        ys, ks, vs, ss = _layer(ys, mod[bp:], wts, past, state_gla[l])
        for lst, val in zip(outs, (kp.reshape(bp, tp, sbh, sbd), vp.reshape(bp, tp, sbh, sbd), sp,
                                   ks.reshape(bs, ts, sbh, sbd), vs.reshape(bs, ts, sbh, sbd), ss)):
            lst.append(val)
    return (yp, ys) + tuple(jnp.stack(v) for v in outs)
```
